```python
import jax, jax.numpy as jnp
from jax import lax
import numpy as np

D_MODEL = 2048
BATCH = 1
SEQ = 8192
DEPTH = 1

N_SUB = 3
D_FF = 5632
POOL_WINDOWS = (2, 4, 8, 16)
POOL_GROUPS = len(POOL_WINDOWS)
POOL_GROUP_W = D_MODEL // 8
POOL_W = POOL_GROUPS * POOL_GROUP_W
HEAD_DIM = 64
N_HEADS = 16
N_KV_HEADS = 2
GQA_GROUP = N_HEADS // N_KV_HEADS
WINDOW = 128
BLK = 128
NUM_BUCKETS = 32
MAX_EXACT = NUM_BUCKETS // 2
REL_MAX_DIST = 128
EPS = 1e-6
NEG_INF = -1e30
IN_SPLITS = (POOL_W, N_HEADS * HEAD_DIM, N_KV_HEADS * HEAD_DIM, N_KV_HEADS * HEAD_DIM, D_MODEL, D_MODEL)
IN_W = sum(IN_SPLITS)

kernel_name = "hybrid_pool_swa_gated_macaron_block"


def rms_norm(x, g):
    xf = x.astype(jnp.float32)
    y = xf * lax.rsqrt(jnp.mean(xf * xf, axis=-1, keepdims=True) + EPS)
    return (y * g.astype(jnp.float32)).astype(x.dtype)


def modulate(h, shift, scale):
    return h * (1 + scale) + shift


def swiglu(h, w_gu, w_down):
    g, u = jnp.split(h @ w_gu, 2, axis=-1)
    return (jax.nn.silu(g) * u) @ w_down


def multiscale_pool(u, pool_mix, pool_scale):
    B, S, _ = u.shape
    uf = u.astype(jnp.float32).reshape(B, S, POOL_GROUPS, POOL_GROUP_W)
    cs = jnp.pad(jnp.cumsum(uf, axis=1), ((0, 0), (1, 0), (0, 0), (0, 0)))
    t1 = np.arange(1, S + 1)
    outs = []
    for gi, w in enumerate(POOL_WINDOWS):
        lo = np.maximum(t1 - w, 0)
        cnt = np.minimum(t1, w).astype(np.float32)[None, :, None]
        win_sum = cs[:, 1:, gi] - cs[:, lo, gi]
        outs.append(win_sum / cnt - uf[:, :, gi])
    pooled = jnp.stack(outs, axis=2).astype(u.dtype)
    mixed = jnp.einsum('bsgc,gcd->bsgd', pooled, pool_mix)
    return mixed.reshape(B, S, POOL_W) * pool_scale


def rel_bucket_band():
    ql = np.arange(BLK)[:, None]
    j = np.arange(2 * BLK)[None, :]
    n = np.clip(BLK + ql - j, 0, None)
    nf = np.maximum(n, 1).astype(np.float32)
    large = MAX_EXACT + (np.log(nf / MAX_EXACT) / np.log(REL_MAX_DIST / MAX_EXACT)
                         * (NUM_BUCKETS - MAX_EXACT)).astype(np.int32)
    large = np.minimum(large, NUM_BUCKETS - 1)
    return np.where(n < MAX_EXACT, n, large).astype(np.int32)


def band_mask(nb):
    qpos = (np.arange(nb)[:, None, None] * BLK + np.arange(BLK)[None, :, None])
    kpos = ((np.arange(nb)[:, None, None] - 1) * BLK + np.arange(2 * BLK)[None, None, :])
    dist = qpos - kpos
    return (dist >= 0) & (dist < WINDOW) & (kpos >= 0)


def swa_sink_attention(q, k, v, q_gain, k_gain, sinks, rel_bias):
    B, S = q.shape[:2]
    nb = S // BLK
    q = rms_norm(q, q_gain)
    k = rms_norm(k, k_gain)
    qb = q.reshape(B, nb, BLK, N_KV_HEADS, GQA_GROUP, HEAD_DIM)

    def band(t):
        tb = t.reshape(B, nb, BLK, N_KV_HEADS, HEAD_DIM)
        prev = jnp.pad(tb, ((0, 0), (1, 0), (0, 0), (0, 0), (0, 0)))[:, :nb]
        return jnp.concatenate([prev, tb], axis=2)

    kband, vband = band(k), band(v)
    logits = jnp.einsum('bnqkgd,bnjkd->bnkgqj', qb, kband).astype(jnp.float32) * (HEAD_DIM ** -0.5)
    bias = rel_bias.astype(jnp.float32)[rel_bucket_band()]
    bias = jnp.transpose(bias, (2, 0, 1)).reshape(N_KV_HEADS, GQA_GROUP, BLK, 2 * BLK)
    logits = logits + bias
    mask = band_mask(nb)[None, :, None, None]
    logits = jnp.where(mask, logits, NEG_INF)
    sink = jnp.broadcast_to(sinks.astype(jnp.float32).reshape(1, 1, N_KV_HEADS, GQA_GROUP, 1, 1),
                            logits.shape[:-1] + (1,))
    p = jax.nn.softmax(jnp.concatenate([logits, sink], axis=-1), axis=-1)[..., :-1]
    out = jnp.einsum('bnkgqj,bnjkd->bnqkgd', p.astype(v.dtype), vband)
    return out.reshape(B, S, N_HEADS * HEAD_DIM)


def setup_inputs(seed: int = 0) -> dict:
    key = jax.random.key(seed)
    ks = jax.random.split(key, 24)
    L = DEPTH

    def dense(k, shape, fan_in):
        return jax.random.normal(k, shape, jnp.float32) * (fan_in ** -0.5)

    def gain(k, shape, s=0.05):
        return 1.0 + s * jax.random.normal(k, shape, jnp.float32)

    return {
        "x": jax.random.normal(ks[0], (BATCH, SEQ, D_MODEL), jnp.float32),
        "c": jax.random.normal(ks[1], (BATCH, D_MODEL), jnp.float32),
        "w_ada": dense(ks[2], (L, D_MODEL, 3 * N_SUB * D_MODEL), D_MODEL),
        "b_ada": 0.02 * jax.random.normal(ks[3], (L, 3 * N_SUB * D_MODEL), jnp.float32),
        "g_ffn1": gain(ks[4], (L, D_MODEL)),
        "w_ffn1_gu": dense(ks[5], (L, D_MODEL, 2 * D_FF), D_MODEL),
        "w_ffn1_down": dense(ks[6], (L, D_FF, D_MODEL), D_FF),
        "g_mix": gain(ks[7], (L, D_MODEL)),
        "w_in": dense(ks[8], (L, D_MODEL, IN_W), D_MODEL),
        "pool_mix": dense(ks[9], (L, POOL_GROUPS, POOL_GROUP_W, POOL_GROUP_W), POOL_GROUP_W),
        "pool_scale": gain(ks[10], (L, POOL_W), 0.1),
        "w_pool_up": dense(ks[11], (L, POOL_W, D_MODEL), POOL_W),
        "q_gain": gain(ks[12], (L, HEAD_DIM)),
        "k_gain": gain(ks[13], (L, HEAD_DIM)),
        "sinks": jax.random.normal(ks[14], (L, N_HEADS), jnp.float32),
        "rel_bias": 0.5 * jax.random.normal(ks[15], (NUM_BUCKETS, N_HEADS), jnp.float32),
        "w_attn_up": dense(ks[16], (L, N_HEADS * HEAD_DIM, D_MODEL), N_HEADS * HEAD_DIM),
        "w_o": dense(ks[17], (L, D_MODEL, D_MODEL), D_MODEL),
        "g_ffn2": gain(ks[18], (L, D_MODEL)),
        "w_ffn2_gu": dense(ks[19], (L, D_MODEL, 2 * D_FF), D_MODEL),
        "w_ffn2_down": dense(ks[20], (L, D_FF, D_MODEL), D_FF),
    }


def reference(x, c, w_ada, b_ada, g_ffn1, w_ffn1_gu, w_ffn1_down, g_mix, w_in, pool_mix,
              pool_scale, w_pool_up, q_gain, k_gain, sinks, rel_bias, w_attn_up, w_o,
              g_ffn2, w_ffn2_gu, w_ffn2_down):
    B, S, D = x.shape
    split_idx = [int(s) for s in np.cumsum(IN_SPLITS)[:-1]]
    for l in range(DEPTH):
        mod = (jax.nn.silu(c) @ w_ada[l] + b_ada[l]).reshape(B, 3 * N_SUB, 1, D)

        h = modulate(rms_norm(x, g_ffn1[l]), mod[:, 0], mod[:, 1])
        x = x + 0.5 * mod[:, 2] * swiglu(h, w_ffn1_gu[l], w_ffn1_down[l])

        h = modulate(rms_norm(x, g_mix[l]), mod[:, 3], mod[:, 4])
        z = h @ w_in[l]
        u_pool, q, k, v, ga, gb = jnp.split(z, split_idx, axis=-1)
        y_pool = multiscale_pool(u_pool, pool_mix[l], pool_scale[l]) @ w_pool_up[l]
        y_attn = swa_sink_attention(
            q.reshape(B, S, N_HEADS, HEAD_DIM),
            k.reshape(B, S, N_KV_HEADS, HEAD_DIM),
            v.reshape(B, S, N_KV_HEADS, HEAD_DIM),
            q_gain[l], k_gain[l], sinks[l], rel_bias) @ w_attn_up[l]
        merged = jax.nn.sigmoid(ga) * y_pool + jax.nn.sigmoid(gb) * y_attn
        x = x + mod[:, 5] * (merged @ w_o[l])

        h = modulate(rms_norm(x, g_ffn2[l]), mod[:, 6], mod[:, 7])
        x = x + 0.5 * mod[:, 8] * swiglu(h, w_ffn2_gu[l], w_ffn2_down[l])
    return x
```

```python
import functools

import jax
import jax.numpy as jnp
import numpy as np
from jax import lax
from jax.experimental import pallas as pl
from jax.experimental.pallas import tpu as pltpu

F32 = jnp.float32
BF16 = jnp.bfloat16

N_SUB = 3
POOL_WINDOWS = (2, 4, 8, 16)
POOL_GROUP_W = 256
POOL_W = len(POOL_WINDOWS) * POOL_GROUP_W
HEAD_DIM = 64
N_HEADS = 16
N_KV_HEADS = 2
GQA_GROUP = N_HEADS // N_KV_HEADS
ATTN_W = N_HEADS * HEAD_DIM
KV_W = 2 * N_KV_HEADS * HEAD_DIM
BLK = 128
NUM_BUCKETS = 32
MAX_EXACT = NUM_BUCKETS // 2
REL_MAX_DIST = 128
EPS = 1e-6
NEG_INF = -1e30
POOL_HALO = 16

V7X_VMEM_BYTES = 64 * 1024 * 1024
VMEM_LIMIT = V7X_VMEM_BYTES - 8 * 1024 * 1024


def _rel_bucket_band():
    ql = np.arange(BLK)[:, None]
    j = np.arange(2 * BLK)[None, :]
    n = np.clip(BLK + ql - j, 0, None)
    nf = np.maximum(n, 1).astype(np.float32)
    large = MAX_EXACT + (np.log(nf / MAX_EXACT) / np.log(REL_MAX_DIST / MAX_EXACT)
                         * (NUM_BUCKETS - MAX_EXACT)).astype(np.int32)
    large = np.minimum(large, NUM_BUCKETS - 1)
    return np.where(n < MAX_EXACT, n, large).astype(np.int32)


def _norm_modulate(x, gain, shift, scale):
    ms = jnp.mean(x * x, axis=-1, keepdims=True)
    y = x * lax.rsqrt(ms + EPS) * gain
    return y * (1.0 + scale) + shift


def _ada_kernel(c_ref, w_ref, b_ref, o_ref):
    cc = c_ref[...]
    sc = cc * jax.nn.sigmoid(cc)
    o_ref[...] = jnp.sum(w_ref[...] * sc, axis=0, keepdims=True) + b_ref[...]


def _ada(c_col, w_ada, b_ada, *, tn):
    d, n = w_ada.shape[1], w_ada.shape[2]
    return pl.pallas_call(
        _ada_kernel,
        grid=(n // tn,),
        in_specs=[
            pl.BlockSpec((d, 1), lambda j: (0, 0)),
            pl.BlockSpec((None, d, tn), lambda j: (0, 0, j)),
            pl.BlockSpec((1, tn), lambda j: (0, j)),
        ],
        out_specs=pl.BlockSpec((1, tn), lambda j: (0, j)),
        out_shape=jax.ShapeDtypeStruct((1, n), F32),
        compiler_params=pltpu.CompilerParams(
            dimension_semantics=("arbitrary",), vmem_limit_bytes=VMEM_LIMIT),
        name="ada_ln",
    )(c_col, w_ada, b_ada)


def _ffn_kernel(x_ref, mod_ref, g_ref, wg_ref, wu_ref, wd_ref, o_ref, h_ref, *, sub):
    j = pl.program_id(1)

    @pl.when(j == 0)
    def _():
        h = _norm_modulate(x_ref[...], g_ref[...],
                           mod_ref[3 * sub:3 * sub + 1, :],
                           mod_ref[3 * sub + 1:3 * sub + 2, :])
        h_ref[...] = h.astype(BF16)
        o_ref[...] = jnp.zeros_like(o_ref)

    h = h_ref[...]
    g = jnp.dot(h, wg_ref[...], preferred_element_type=F32)
    u = jnp.dot(h, wu_ref[...], preferred_element_type=F32)
    a = (g * jax.nn.sigmoid(g) * u).astype(BF16)
    o_ref[...] += jnp.dot(a, wd_ref[...], preferred_element_type=F32)

    @pl.when(j == pl.num_programs(1) - 1)
    def _():
        gate = mod_ref[3 * sub + 2:3 * sub + 3, :]
        o_ref[...] = x_ref[...] + (0.5 * gate) * o_ref[...]


def _ffn(x, mod, gain, w_gu, w_down, *, sub, tm, tf):
    s, d = x.shape
    dff = w_down.shape[0]
    nf = dff // tf
    return pl.pallas_call(
        functools.partial(_ffn_kernel, sub=sub),
        grid=(s // tm, nf),
        in_specs=[
            pl.BlockSpec((tm, d), lambda i, j: (i, 0)),
            pl.BlockSpec(mod.shape, lambda i, j: (0, 0)),
            pl.BlockSpec((1, d), lambda i, j: (0, 0)),
            pl.BlockSpec((d, tf), lambda i, j: (0, j)),
            pl.BlockSpec((d, tf), lambda i, j: (0, j + nf)),
            pl.BlockSpec((tf, d), lambda i, j: (j, 0)),
        ],
        out_specs=pl.BlockSpec((tm, d), lambda i, j: (i, 0)),
        out_shape=jax.ShapeDtypeStruct((s, d), F32),
        scratch_shapes=[pltpu.VMEM((tm, d), BF16)],
        compiler_params=pltpu.CompilerParams(
            dimension_semantics=("parallel", "arbitrary"), vmem_limit_bytes=VMEM_LIMIT),
        name=f"ffn{sub}",
    )(x, mod, gain, w_gu, w_gu, w_down)


def _inproj_kernel(x_ref, mod_ref, g_ref, w_ref, o_ref, h_ref, *, sub):
    @pl.when(pl.program_id(1) == 0)
    def _():
        h = _norm_modulate(x_ref[...], g_ref[...],
                           mod_ref[3 * sub:3 * sub + 1, :],
                           mod_ref[3 * sub + 1:3 * sub + 2, :])
        h_ref[...] = h.astype(BF16)

    o_ref[...] = jnp.dot(h_ref[...], w_ref[...], preferred_element_type=F32)


def _inproj(x, mod, gain, w, *, sub, tm, tn):
    s, d = x.shape
    n = w.shape[1]
    return pl.pallas_call(
        functools.partial(_inproj_kernel, sub=sub),
        grid=(s // tm, n // tn),
        in_specs=[
            pl.BlockSpec((tm, d), lambda i, j: (i, 0)),
            pl.BlockSpec(mod.shape, lambda i, j: (0, 0)),
            pl.BlockSpec((1, d), lambda i, j: (0, 0)),
            pl.BlockSpec((d, tn), lambda i, j: (0, j)),
        ],
        out_specs=pl.BlockSpec((tm, tn), lambda i, j: (i, j)),
        out_shape=jax.ShapeDtypeStruct((s, n), F32),
        scratch_shapes=[pltpu.VMEM((tm, d), BF16)],
        compiler_params=pltpu.CompilerParams(
            dimension_semantics=("parallel", "arbitrary"), vmem_limit_bytes=VMEM_LIMIT),
        name="in_proj",
    )(x, mod, gain, w)


def _head_rms(t, gain):
    ms = jnp.mean(t * t, axis=-1, keepdims=True)
    return t * lax.rsqrt(ms + EPS) * gain


def _attn_kernel(q_ref, kv_ref, kvp_ref, qg_ref, kg_ref, bucket_ref, rel_ref, sink_ref,
                 o_ref, bias_scr, sink_scr, kv_scr, *, nblk):
    i = pl.program_id(0)

    @pl.when(i == 0)
    def _():
        bucket = bucket_ref[...]
        ql = lax.broadcasted_iota(jnp.int32, (BLK, 2 * BLK), 0)
        jj = lax.broadcasted_iota(jnp.int32, (BLK, 2 * BLK), 1)
        dist = BLK + ql - jj
        visible = jnp.logical_and(dist >= 0, dist < BLK)
        for h in range(N_HEADS):
            acc = jnp.zeros((BLK, 2 * BLK), F32)
            for b in range(NUM_BUCKETS):
                acc = jnp.where(bucket == b, rel_ref[b, h], acc)
            bias_scr[h * BLK:(h + 1) * BLK, :] = jnp.where(visible, acc, NEG_INF)
            sink_scr[h * BLK:(h + 1) * BLK, :] = jnp.full((BLK, 1), sink_ref[h], F32)

    kv_scr[0:BLK, :] = kvp_ref[...]
    kv_scr[BLK:, :] = kv_ref[...]
    qg = qg_ref[...] * (HEAD_DIM ** -0.5)
    kg = kg_ref[...]
    jj = lax.broadcasted_iota(jnp.int32, (GQA_GROUP * BLK, 2 * BLK), 1)

    def block(b, carry):
        r0 = pl.multiple_of(b * BLK, BLK)
        band = kv_scr[pl.ds(r0, 2 * BLK), :]
        qb = q_ref[pl.ds(r0, BLK), :]
        hidden = jnp.where(jnp.logical_and(i == 0, b == 0), BLK, 0)
        for kh in range(N_KV_HEADS):
            kb = _head_rms(band[:, kh * HEAD_DIM:(kh + 1) * HEAD_DIM], kg).astype(BF16)
            vb = band[:, (N_KV_HEADS + kh) * HEAD_DIM:
                      (N_KV_HEADS + kh + 1) * HEAD_DIM].astype(BF16)
            qs = jnp.concatenate(
                [qb[:, (kh * GQA_GROUP + g) * HEAD_DIM:(kh * GQA_GROUP + g + 1) * HEAD_DIM]
                 for g in range(GQA_GROUP)], axis=0)
            qs = _head_rms(qs, qg).astype(BF16)
            s = lax.dot_general(qs, kb, (((1,), (1,)), ((), ())),
                                preferred_element_type=F32)
            rows = slice(kh * GQA_GROUP * BLK, (kh + 1) * GQA_GROUP * BLK)
            s = s + bias_scr[rows, :]
            s = jnp.where(jj < hidden, NEG_INF, s)
            sink = sink_scr[rows, :]
            m = jnp.maximum(jnp.max(s, axis=-1, keepdims=True), sink)
            e = jnp.exp(s - m)
            denom = jnp.sum(e, axis=-1, keepdims=True) + jnp.exp(sink - m)
            p = (e / denom).astype(BF16)
            o = jnp.dot(p, vb, preferred_element_type=F32)
            o = jnp.concatenate([o[g * BLK:(g + 1) * BLK, :] for g in range(GQA_GROUP)],
                                axis=1)
            o_ref[pl.ds(r0, BLK), kh * GQA_GROUP * HEAD_DIM:
                  (kh + 1) * GQA_GROUP * HEAD_DIM] = o.astype(BF16)
        return carry

    lax.fori_loop(0, nblk, block, 0)


def _attention(z, q_gain, k_gain, rel_bias, sinks, *, tq):
    s = z.shape[0]
    nblk = tq // BLK
    q_col = POOL_W // ATTN_W
    kv_col = (z.shape[1] - KV_W) // KV_W
    bucket = jnp.asarray(_rel_bucket_band())
    return pl.pallas_call(
        functools.partial(_attn_kernel, nblk=nblk),
        grid=(s // tq,),
        in_specs=[
            pl.BlockSpec((tq, ATTN_W), lambda i: (i, q_col)),
            pl.BlockSpec((tq, KV_W), lambda i: (i, kv_col)),
            pl.BlockSpec((BLK, KV_W), lambda i: (jnp.maximum(i * nblk - 1, 0), kv_col)),
            pl.BlockSpec((1, HEAD_DIM), lambda i: (0, 0)),
            pl.BlockSpec((1, HEAD_DIM), lambda i: (0, 0)),
            pl.BlockSpec((BLK, 2 * BLK), lambda i: (0, 0)),
            pl.BlockSpec(memory_space=pltpu.SMEM),
            pl.BlockSpec(memory_space=pltpu.SMEM),
        ],
        out_specs=pl.BlockSpec((tq, ATTN_W), lambda i: (i, 0)),
        out_shape=jax.ShapeDtypeStruct((s, ATTN_W), BF16),
        scratch_shapes=[
            pltpu.VMEM((N_HEADS * BLK, 2 * BLK), F32),
            pltpu.VMEM((N_HEADS * BLK, 1), F32),
            pltpu.VMEM((tq + BLK, KV_W), F32),
        ],
        compiler_params=pltpu.CompilerParams(
            dimension_semantics=("arbitrary",), vmem_limit_bytes=VMEM_LIMIT),
        name="swa_attention",
    )(z, z, z, q_gain, k_gain, bucket, rel_bias, sinks)


def _mix_kernel(u_ref, up_ref, ga_ref, gb_ref, attn_ref, x_ref, mod_ref, pmix_ref,
                pscale_ref, wpool_ref, wattn_ref, wo_ref, o_ref, ubuf, *, sub, tm):
    i = pl.program_id(0)
    halo = up_ref[...]
    ubuf[0:POOL_HALO, :] = jnp.where(i == 0, 0.0, halo)
    ubuf[POOL_HALO:, :] = u_ref[...]

    t1 = i * tm + lax.broadcasted_iota(jnp.int32, (tm, POOL_GROUP_W), 0) + 1
    mixed = []
    for gi, w in enumerate(POOL_WINDOWS):
        cols = slice(gi * POOL_GROUP_W, (gi + 1) * POOL_GROUP_W)
        cur = ubuf[POOL_HALO:POOL_HALO + tm, cols]
        win = cur
        for dlt in range(1, w):
            win = win + ubuf[POOL_HALO - dlt:POOL_HALO - dlt + tm, cols]
        cnt = jnp.minimum(t1, w).astype(F32)
        pooled = (win / cnt - cur).astype(BF16)
        mg = jnp.dot(pooled, pmix_ref[gi], preferred_element_type=F32)
        mixed.append((mg * pscale_ref[:, cols]).astype(BF16))
    mixed = jnp.concatenate(mixed, axis=1)

    y_pool = jnp.dot(mixed, wpool_ref[...], preferred_element_type=F32)
    y_attn = jnp.dot(attn_ref[...], wattn_ref[...], preferred_element_type=F32)
    merged = jax.nn.sigmoid(ga_ref[...]) * y_pool + jax.nn.sigmoid(gb_ref[...]) * y_attn
    y = jnp.dot(merged.astype(BF16), wo_ref[...], preferred_element_type=F32)
    gate = mod_ref[3 * sub + 2:3 * sub + 3, :]
    o_ref[...] = x_ref[...] + gate * y


def _mix(z, attn, x, mod, pool_mix, pool_scale, w_pool_up, w_attn_up, w_o, *, sub, tm):
    s, d = x.shape
    const = dict(pipeline_mode=pl.Buffered(1))
    ga_col = (POOL_W + ATTN_W) // d
    return pl.pallas_call(
        functools.partial(_mix_kernel, sub=sub, tm=tm),
        grid=(s // tm,),
        in_specs=[
            pl.BlockSpec((tm, POOL_W), lambda i: (i, 0)),
            pl.BlockSpec((POOL_HALO, POOL_W),
                         lambda i: (jnp.maximum(i * (tm // POOL_HALO) - 1, 0), 0)),
            pl.BlockSpec((tm, d), lambda i: (i, ga_col)),
            pl.BlockSpec((tm, d), lambda i: (i, ga_col + 1)),
            pl.BlockSpec((tm, ATTN_W), lambda i: (i, 0)),
            pl.BlockSpec((tm, d), lambda i: (i, 0)),
            pl.BlockSpec(mod.shape, lambda i: (0, 0), **const),
            pl.BlockSpec(pool_mix.shape, lambda i: (0, 0, 0), **const),
            pl.BlockSpec((1, POOL_W), lambda i: (0, 0), **const),
            pl.BlockSpec(w_pool_up.shape, lambda i: (0, 0), **const),
            pl.BlockSpec(w_attn_up.shape, lambda i: (0, 0), **const),
            pl.BlockSpec(w_o.shape, lambda i: (0, 0), **const),
        ],
        out_specs=pl.BlockSpec((tm, d), lambda i: (i, 0)),
        out_shape=jax.ShapeDtypeStruct((s, d), F32),
        scratch_shapes=[pltpu.VMEM((tm + POOL_HALO, POOL_W), F32)],
        compiler_params=pltpu.CompilerParams(
            dimension_semantics=("arbitrary",), vmem_limit_bytes=VMEM_LIMIT),
        name="pool_merge_out",
    )(z, z, z, z, attn, x, mod, pool_mix, pool_scale, w_pool_up, w_attn_up, w_o)


def kernel(x, c, w_ada, b_ada, g_ffn1, w_ffn1_gu, w_ffn1_down, g_mix, w_in, pool_mix,
           pool_scale, w_pool_up, q_gain, k_gain, sinks, rel_bias, w_attn_up, w_o,
           g_ffn2, w_ffn2_gu, w_ffn2_down):
    b, s, d = x.shape
    depth = w_ada.shape[0]
    assert b == 1, "adaLN matvec and row tiling assume a single sequence"
    xs = x[0]
    gate_lo = POOL_W + ATTN_W
    gate_hi = gate_lo + KV_W
    for l in range(depth):
        mod = _ada(c.reshape(d, 1), w_ada[l:l + 1], b_ada[l:l + 1], tn=1024)
        mod = mod.reshape(3 * N_SUB, d)

        xs = _ffn(xs, mod, g_ffn1[l:l + 1], w_ffn1_gu[l].astype(BF16),
                  w_ffn1_down[l].astype(BF16), sub=0, tm=512, tf=512)

        w_in_l = w_in[l]
        w_in_r = jnp.concatenate(
            [w_in_l[:, :gate_lo], w_in_l[:, gate_hi:], w_in_l[:, gate_lo:gate_hi]],
            axis=1).astype(BF16)
        z = _inproj(xs, mod, g_mix[l:l + 1], w_in_r, sub=1, tm=1024, tn=1280)
        attn = _attention(z, q_gain[l:l + 1], k_gain[l:l + 1], rel_bias, sinks[l], tq=512)
        xs = _mix(z, attn, xs, mod, pool_mix[l].astype(BF16), pool_scale[l:l + 1],
                  w_pool_up[l].astype(BF16), w_attn_up[l].astype(BF16),
                  w_o[l].astype(BF16), sub=1, tm=256)

        xs = _ffn(xs, mod, g_ffn2[l:l + 1], w_ffn2_gu[l].astype(BF16),
                  w_ffn2_down[l].astype(BF16), sub=2, tm=512, tf=512)
    return xs[None]
```

```python
import functools

import jax
import jax.numpy as jnp
import numpy as np
from jax import lax
from jax.experimental import pallas as pl
from jax.experimental.pallas import tpu as pltpu

F32 = jnp.float32
BF16 = jnp.bfloat16

N_SUB = 3
POOL_WINDOWS = (2, 4, 8, 16)
POOL_GROUP_W = 256
POOL_W = len(POOL_WINDOWS) * POOL_GROUP_W
HEAD_DIM = 64
N_HEADS = 16
N_KV_HEADS = 2
GQA_GROUP = N_HEADS // N_KV_HEADS
ATTN_W = N_HEADS * HEAD_DIM
KV_W = 2 * N_KV_HEADS * HEAD_DIM
BLK = 128
NUM_BUCKETS = 32
MAX_EXACT = NUM_BUCKETS // 2
REL_MAX_DIST = 128
EPS = 1e-6
NEG_INF = -1e30
LOG2E = float(np.log2(np.e))
POOL_HALO = 16

V7X_VMEM_BYTES = 64 * 1024 * 1024
VMEM_LIMIT = V7X_VMEM_BYTES - 8 * 1024 * 1024


def _rel_bucket_band():
    ql = np.arange(BLK)[:, None]
    j = np.arange(2 * BLK)[None, :]
    n = np.clip(BLK + ql - j, 0, None)
    nf = np.maximum(n, 1).astype(np.float32)
    large = MAX_EXACT + (np.log(nf / MAX_EXACT) / np.log(REL_MAX_DIST / MAX_EXACT)
                         * (NUM_BUCKETS - MAX_EXACT)).astype(np.int32)
    large = np.minimum(large, NUM_BUCKETS - 1)
    return np.where(n < MAX_EXACT, n, large).astype(np.int32)


def _norm_modulate(x, gain, shift, scale):
    ms = jnp.mean(x * x, axis=-1, keepdims=True)
    y = x * lax.rsqrt(ms + EPS) * gain
    return y * (1.0 + scale) + shift


def _ada_kernel(c_ref, w_ref, b_ref, o_ref):
    cc = c_ref[...]
    sc = cc * jax.nn.sigmoid(cc)
    o_ref[...] = jnp.sum(w_ref[...] * sc, axis=0, keepdims=True) + b_ref[...]


def _ada(c_col, w_ada, b_ada, *, tn):
    d, n = w_ada.shape[1], w_ada.shape[2]
    return pl.pallas_call(
        _ada_kernel,
        grid=(n // tn,),
        in_specs=[
            pl.BlockSpec((d, 1), lambda j: (0, 0)),
            pl.BlockSpec((None, d, tn), lambda j: (0, 0, j)),
            pl.BlockSpec((1, tn), lambda j: (0, j)),
        ],
        out_specs=pl.BlockSpec((1, tn), lambda j: (0, j)),
        out_shape=jax.ShapeDtypeStruct((1, n), F32),
        compiler_params=pltpu.CompilerParams(
            dimension_semantics=("arbitrary",), vmem_limit_bytes=VMEM_LIMIT),
        name="ada_ln",
    )(c_col, w_ada, b_ada)


def _ffn_kernel(x_ref, mod_ref, g_ref, wg_ref, wu_ref, wd_ref, o_ref, h_ref, *, sub):
    j = pl.program_id(1)

    @pl.when(j == 0)
    def _():
        h = _norm_modulate(x_ref[...], g_ref[...],
                           mod_ref[3 * sub:3 * sub + 1, :],
                           mod_ref[3 * sub + 1:3 * sub + 2, :])
        h_ref[...] = h.astype(BF16)
        o_ref[...] = jnp.zeros_like(o_ref)

    h = h_ref[...]
    g = jnp.dot(h, wg_ref[...], preferred_element_type=F32)
    u = jnp.dot(h, wu_ref[...], preferred_element_type=F32)
    a = (g * jax.nn.sigmoid(g) * u).astype(BF16)
    o_ref[...] += jnp.dot(a, wd_ref[...], preferred_element_type=F32)

    @pl.when(j == pl.num_programs(1) - 1)
    def _():
        gate = mod_ref[3 * sub + 2:3 * sub + 3, :]
        o_ref[...] = x_ref[...] + (0.5 * gate) * o_ref[...]


def _ffn(x, mod, gain, w_gu, w_down, *, sub, tm, tf):
    s, d = x.shape
    dff = w_down.shape[0]
    nf = dff // tf
    return pl.pallas_call(
        functools.partial(_ffn_kernel, sub=sub),
        grid=(s // tm, nf),
        in_specs=[
            pl.BlockSpec((tm, d), lambda i, j: (i, 0)),
            pl.BlockSpec(mod.shape, lambda i, j: (0, 0)),
            pl.BlockSpec((1, d), lambda i, j: (0, 0)),
            pl.BlockSpec((d, tf), lambda i, j: (0, j)),
            pl.BlockSpec((d, tf), lambda i, j: (0, j + nf)),
            pl.BlockSpec((tf, d), lambda i, j: (j, 0)),
        ],
        out_specs=pl.BlockSpec((tm, d), lambda i, j: (i, 0)),
        out_shape=jax.ShapeDtypeStruct((s, d), F32),
        scratch_shapes=[pltpu.VMEM((tm, d), BF16)],
        compiler_params=pltpu.CompilerParams(
            dimension_semantics=("parallel", "arbitrary"), vmem_limit_bytes=VMEM_LIMIT),
        name=f"ffn{sub}",
    )(x, mod, gain, w_gu, w_gu, w_down)


def _inproj_kernel(x_ref, mod_ref, g_ref, w_ref, o_ref, h_ref, *, sub):
    @pl.when(pl.program_id(1) == 0)
    def _():
        h = _norm_modulate(x_ref[...], g_ref[...],
                           mod_ref[3 * sub:3 * sub + 1, :],
                           mod_ref[3 * sub + 1:3 * sub + 2, :])
        h_ref[...] = h.astype(BF16)

    o_ref[...] = jnp.dot(h_ref[...], w_ref[...], preferred_element_type=F32)


def _inproj(x, mod, gain, w, *, sub, tm, tn):
    s, d = x.shape
    n = w.shape[1]
    return pl.pallas_call(
        functools.partial(_inproj_kernel, sub=sub),
        grid=(s // tm, n // tn),
        in_specs=[
            pl.BlockSpec((tm, d), lambda i, j: (i, 0)),
            pl.BlockSpec(mod.shape, lambda i, j: (0, 0)),
            pl.BlockSpec((1, d), lambda i, j: (0, 0)),
            pl.BlockSpec((d, tn), lambda i, j: (0, j)),
        ],
        out_specs=pl.BlockSpec((tm, tn), lambda i, j: (i, j)),
        out_shape=jax.ShapeDtypeStruct((s, n), F32),
        scratch_shapes=[pltpu.VMEM((tm, d), BF16)],
        compiler_params=pltpu.CompilerParams(
            dimension_semantics=("parallel", "arbitrary"), vmem_limit_bytes=VMEM_LIMIT),
        name="in_proj",
    )(x, mod, gain, w)


def _rms_over_rows(t):
    ms = jnp.mean(t * t, axis=1, keepdims=True)
    return t * lax.rsqrt(ms + EPS)


def _attn_kernel(q_ref, kv_ref, kvp_ref, qg_ref, kg_ref, bucket_ref, rel_ref, sink_ref,
                 o_ref, bias_scr, sink_scr, *, nblk):
    i = pl.program_id(0)

    from_prev = (lax.broadcasted_iota(jnp.int32, (BLK, BLK), 0)
                 > lax.broadcasted_iota(jnp.int32, (BLK, BLK), 1))

    @pl.when(i == 0)
    def _():
        bucket = bucket_ref[...]
        for h in range(N_HEADS):
            kh, g = divmod(h, GQA_GROUP)
            lanes = slice(g * BLK, (g + 1) * BLK)
            acc = jnp.zeros((BLK, BLK), F32)
            for b in range(NUM_BUCKETS):
                acc = jnp.where(bucket == b, rel_ref[b, h] * LOG2E, acc)
            bias_scr[kh, :, lanes] = acc
            bias_scr[N_KV_HEADS + kh, :, lanes] = jnp.where(from_prev, NEG_INF, acc)
            sink_scr[kh, :, lanes] = jnp.full((1, BLK), sink_ref[h] * LOG2E, F32)

    t = (nblk + 1) * BLK
    kv_t = jnp.concatenate([kvp_ref[...], kv_ref[...]], axis=0).T
    k_t = kv_t[0:N_KV_HEADS * HEAD_DIM].reshape(N_KV_HEADS, HEAD_DIM, t)
    k_gain = kg_ref[...] * qg_ref[...] * (HEAD_DIM ** -0.5 * LOG2E)
    kn = (_rms_over_rows(k_t) * k_gain).reshape(N_KV_HEADS * HEAD_DIM, t).T
    lane = lax.broadcasted_iota(jnp.int32, kn.shape, 1)
    k_sel = [jnp.where((lane // HEAD_DIM) == kh, kn, 0.0).astype(BF16)
             for kh in range(N_KV_HEADS)]
    v_t = [kv_t[(N_KV_HEADS + kh) * HEAD_DIM:(N_KV_HEADS + kh + 1) * HEAD_DIM].astype(BF16)
           for kh in range(N_KV_HEADS)]

    for b in range(nblk):
        rows = slice(b * BLK, (b + 1) * BLK)
        band = slice(b * BLK, (b + 2) * BLK)
        q_t = q_ref[rows, :].T.reshape(N_HEADS, HEAD_DIM, BLK)
        qn = _rms_over_rows(q_t).astype(BF16)
        out_t = []
        for kh in range(N_KV_HEADS):
            q_grp = jnp.concatenate([qn[kh * GQA_GROUP + g] for g in range(GQA_GROUP)],
                                    axis=1)
            q_dup = jnp.concatenate([q_grp] * N_KV_HEADS, axis=0)
            s = jnp.dot(k_sel[kh][band], q_dup, preferred_element_type=F32)
            table = kh + N_KV_HEADS * (i == 0).astype(jnp.int32) if b == 0 else kh
            e_parts, denom_parts = [], []
            for g in range(GQA_GROUP):
                lanes = slice(g * BLK, (g + 1) * BLK)
                sh = jnp.where(from_prev, s[0:BLK, lanes], s[BLK:2 * BLK, lanes])
                sh = sh + bias_scr[table, :, lanes]
                sink = sink_scr[kh, :, lanes]
                m = jnp.maximum(jnp.max(sh, axis=0, keepdims=True), sink)
                e = jnp.exp2(sh - m)
                denom_parts.append(jnp.sum(e, axis=0, keepdims=True) + jnp.exp2(sink - m))
                e_parts.append(jnp.concatenate(
                    [jnp.where(from_prev, e, 0.0), jnp.where(from_prev, 0.0, e)],
                    axis=0).astype(BF16))
            e = jnp.concatenate(e_parts, axis=1)
            denom = jnp.concatenate(denom_parts, axis=1)
            o = jnp.dot(v_t[kh][:, band], e, preferred_element_type=F32)
            o = o * (1.0 / denom)
            out_t += [o[:, g * BLK:(g + 1) * BLK] for g in range(GQA_GROUP)]
        o_ref[rows, :] = jnp.concatenate(out_t, axis=0).T.astype(BF16)


def _attention(z, q_gain, k_gain, rel_bias, sinks, *, tq):
    s = z.shape[0]
    nblk = tq // BLK
    gw = GQA_GROUP * BLK
    q_col = POOL_W // ATTN_W
    kv_col = (z.shape[1] - KV_W) // KV_W
    bucket_t = _rel_bucket_band().T
    jm, ql = np.meshgrid(np.arange(BLK), np.arange(BLK), indexing="ij")
    bucket_fold = jnp.asarray(np.where(jm > ql, bucket_t[:BLK], bucket_t[BLK:]))
    return pl.pallas_call(
        functools.partial(_attn_kernel, nblk=nblk),
        grid=(s // tq,),
        in_specs=[
            pl.BlockSpec((tq, ATTN_W), lambda i: (i, q_col)),
            pl.BlockSpec((tq, KV_W), lambda i: (i, kv_col)),
            pl.BlockSpec((BLK, KV_W), lambda i: (jnp.maximum(i * nblk - 1, 0), kv_col)),
            pl.BlockSpec((HEAD_DIM, 1), lambda i: (0, 0)),
            pl.BlockSpec((HEAD_DIM, 1), lambda i: (0, 0)),
            pl.BlockSpec((BLK, BLK), lambda i: (0, 0)),
            pl.BlockSpec(memory_space=pltpu.SMEM),
            pl.BlockSpec(memory_space=pltpu.SMEM),
        ],
        out_specs=pl.BlockSpec((tq, ATTN_W), lambda i: (i, 0)),
        out_shape=jax.ShapeDtypeStruct((s, ATTN_W), BF16),
        scratch_shapes=[
            pltpu.VMEM((2 * N_KV_HEADS, BLK, gw), F32),
            pltpu.VMEM((N_KV_HEADS, 1, gw), F32),
        ],
        compiler_params=pltpu.CompilerParams(
            dimension_semantics=("arbitrary",), vmem_limit_bytes=VMEM_LIMIT),
        name="swa_attention",
    )(z, z, z, q_gain.reshape(HEAD_DIM, 1), k_gain.reshape(HEAD_DIM, 1), bucket_fold,
      rel_bias, sinks)


def _mix_kernel(u_ref, up_ref, ga_ref, gb_ref, attn_ref, x_ref, mod_ref, pmix_ref,
                pscale_ref, wpool_ref, wattn_ref, wo_ref, o_ref, ubuf, *, sub, tm):
    i = pl.program_id(0)
    halo = up_ref[...]
    ubuf[0:POOL_HALO, :] = jnp.where(i == 0, 0.0, halo)
    ubuf[POOL_HALO:, :] = u_ref[...]

    t1 = i * tm + lax.broadcasted_iota(jnp.int32, (tm, POOL_GROUP_W), 0) + 1
    mixed = []
    for gi, w in enumerate(POOL_WINDOWS):
        cols = slice(gi * POOL_GROUP_W, (gi + 1) * POOL_GROUP_W)
        cur = ubuf[POOL_HALO:POOL_HALO + tm, cols]
        win = cur
        for dlt in range(1, w):
            win = win + ubuf[POOL_HALO - dlt:POOL_HALO - dlt + tm, cols]
        cnt = jnp.minimum(t1, w).astype(F32)
        pooled = (win / cnt - cur).astype(BF16)
        mg = jnp.dot(pooled, pmix_ref[gi], preferred_element_type=F32)
        mixed.append((mg * pscale_ref[:, cols]).astype(BF16))
    mixed = jnp.concatenate(mixed, axis=1)

    y_pool = jnp.dot(mixed, wpool_ref[...], preferred_element_type=F32)
    y_attn = jnp.dot(attn_ref[...], wattn_ref[...], preferred_element_type=F32)
    merged = jax.nn.sigmoid(ga_ref[...]) * y_pool + jax.nn.sigmoid(gb_ref[...]) * y_attn
    y = jnp.dot(merged.astype(BF16), wo_ref[...], preferred_element_type=F32)
    gate = mod_ref[3 * sub + 2:3 * sub + 3, :]
    o_ref[...] = x_ref[...] + gate * y


def _mix(z, attn, x, mod, pool_mix, pool_scale, w_pool_up, w_attn_up, w_o, *, sub, tm):
    s, d = x.shape
    const = dict(pipeline_mode=pl.Buffered(1))
    ga_col = (POOL_W + ATTN_W) // d
    return pl.pallas_call(
        functools.partial(_mix_kernel, sub=sub, tm=tm),
        grid=(s // tm,),
        in_specs=[
            pl.BlockSpec((tm, POOL_W), lambda i: (i, 0)),
            pl.BlockSpec((POOL_HALO, POOL_W),
                         lambda i: (jnp.maximum(i * (tm // POOL_HALO) - 1, 0), 0)),
            pl.BlockSpec((tm, d), lambda i: (i, ga_col)),
            pl.BlockSpec((tm, d), lambda i: (i, ga_col + 1)),
            pl.BlockSpec((tm, ATTN_W), lambda i: (i, 0)),
            pl.BlockSpec((tm, d), lambda i: (i, 0)),
            pl.BlockSpec(mod.shape, lambda i: (0, 0), **const),
            pl.BlockSpec(pool_mix.shape, lambda i: (0, 0, 0), **const),
            pl.BlockSpec((1, POOL_W), lambda i: (0, 0), **const),
            pl.BlockSpec(w_pool_up.shape, lambda i: (0, 0), **const),
            pl.BlockSpec(w_attn_up.shape, lambda i: (0, 0), **const),
            pl.BlockSpec(w_o.shape, lambda i: (0, 0), **const),
        ],
        out_specs=pl.BlockSpec((tm, d), lambda i: (i, 0)),
        out_shape=jax.ShapeDtypeStruct((s, d), F32),
        scratch_shapes=[pltpu.VMEM((tm + POOL_HALO, POOL_W), F32)],
        compiler_params=pltpu.CompilerParams(
            dimension_semantics=("arbitrary",), vmem_limit_bytes=VMEM_LIMIT),
        name="pool_merge_out",
    )(z, z, z, z, attn, x, mod, pool_mix, pool_scale, w_pool_up, w_attn_up, w_o)


def kernel(x, c, w_ada, b_ada, g_ffn1, w_ffn1_gu, w_ffn1_down, g_mix, w_in, pool_mix,
           pool_scale, w_pool_up, q_gain, k_gain, sinks, rel_bias, w_attn_up, w_o,
           g_ffn2, w_ffn2_gu, w_ffn2_down):
    b, s, d = x.shape
    depth = w_ada.shape[0]
    assert b == 1, "adaLN matvec and row tiling assume a single sequence"
    xs = x[0]
    gate_lo = POOL_W + ATTN_W
    gate_hi = gate_lo + KV_W
    for l in range(depth):
        mod = _ada(c.reshape(d, 1), w_ada[l:l + 1], b_ada[l:l + 1], tn=1024)
        mod = mod.reshape(3 * N_SUB, d)

        xs = _ffn(xs, mod, g_ffn1[l:l + 1], w_ffn1_gu[l].astype(BF16),
                  w_ffn1_down[l].astype(BF16), sub=0, tm=512, tf=512)

        w_in_l = w_in[l]
        w_in_r = jnp.concatenate(
            [w_in_l[:, :gate_lo], w_in_l[:, gate_hi:], w_in_l[:, gate_lo:gate_hi]],
            axis=1).astype(BF16)
        z = _inproj(xs, mod, g_mix[l:l + 1], w_in_r, sub=1, tm=1024, tn=1280)
        attn = _attention(z, q_gain[l], k_gain[l], rel_bias, sinks[l], tq=512)
        xs = _mix(z, attn, xs, mod, pool_mix[l].astype(BF16), pool_scale[l:l + 1],
                  w_pool_up[l].astype(BF16), w_attn_up[l].astype(BF16),
                  w_o[l].astype(BF16), sub=1, tm=256)

        xs = _ffn(xs, mod, g_ffn2[l:l + 1], w_ffn2_gu[l].astype(BF16),
                  w_ffn2_down[l].astype(BF16), sub=2, tm=512, tf=512)
    return xs[None]
```

```python
import functools

import jax
import jax.numpy as jnp
import numpy as np
from jax import lax
from jax.experimental import pallas as pl
from jax.experimental.pallas import tpu as pltpu

F32 = jnp.float32
BF16 = jnp.bfloat16

N_SUB = 3
POOL_WINDOWS = (2, 4, 8, 16)
POOL_GROUP_W = 256
POOL_W = len(POOL_WINDOWS) * POOL_GROUP_W
HEAD_DIM = 64
N_HEADS = 16
N_KV_HEADS = 2
GQA_GROUP = N_HEADS // N_KV_HEADS
ATTN_W = N_HEADS * HEAD_DIM
KV_W = 2 * N_KV_HEADS * HEAD_DIM
BLK = 128
NUM_BUCKETS = 32
MAX_EXACT = NUM_BUCKETS // 2
REL_MAX_DIST = 128
EPS = 1e-6
NEG_INF = -1e30
LOG2E = float(np.log2(np.e))
POOL_HALO = 16

V7X_VMEM_BYTES = 64 * 1024 * 1024
VMEM_LIMIT = V7X_VMEM_BYTES - 8 * 1024 * 1024


def _rel_bucket_band():
    ql = np.arange(BLK)[:, None]
    j = np.arange(2 * BLK)[None, :]
    n = np.clip(BLK + ql - j, 0, None)
    nf = np.maximum(n, 1).astype(np.float32)
    large = MAX_EXACT + (np.log(nf / MAX_EXACT) / np.log(REL_MAX_DIST / MAX_EXACT)
                         * (NUM_BUCKETS - MAX_EXACT)).astype(np.int32)
    large = np.minimum(large, NUM_BUCKETS - 1)
    return np.where(n < MAX_EXACT, n, large).astype(np.int32)


def _norm_modulate(x, gain, shift, scale):
    ms = jnp.mean(x * x, axis=-1, keepdims=True)
    y = x * lax.rsqrt(ms + EPS) * gain
    return y * (1.0 + scale) + shift


def _ada_kernel(c_ref, w_ref, b_ref, o_ref):
    cc = c_ref[...]
    sc = cc * jax.nn.sigmoid(cc)
    o_ref[...] = jnp.sum(w_ref[...] * sc, axis=0, keepdims=True) + b_ref[...]


def _ada(c_col, w_ada, b_ada, *, tn):
    d, n = w_ada.shape[1], w_ada.shape[2]
    return pl.pallas_call(
        _ada_kernel,
        grid=(n // tn,),
        in_specs=[
            pl.BlockSpec((d, 1), lambda j: (0, 0)),
            pl.BlockSpec((None, d, tn), lambda j: (0, 0, j)),
            pl.BlockSpec((1, tn), lambda j: (0, j)),
        ],
        out_specs=pl.BlockSpec((1, tn), lambda j: (0, j)),
        out_shape=jax.ShapeDtypeStruct((1, n), F32),
        compiler_params=pltpu.CompilerParams(
            dimension_semantics=("arbitrary",), vmem_limit_bytes=VMEM_LIMIT),
        name="ada_ln",
    )(c_col, w_ada, b_ada)


def _ffn_kernel(x_ref, mod_ref, g_ref, wg_ref, wu_ref, wd_ref, o_ref, h_ref, *, sub):
    j = pl.program_id(1)

    @pl.when(j == 0)
    def _():
        h = _norm_modulate(x_ref[...], g_ref[...],
                           mod_ref[3 * sub:3 * sub + 1, :],
                           mod_ref[3 * sub + 1:3 * sub + 2, :])
        h_ref[...] = h.astype(BF16)
        o_ref[...] = jnp.zeros_like(o_ref)

    h = h_ref[...]
    g = jnp.dot(h, wg_ref[...].astype(BF16), preferred_element_type=F32)
    u = jnp.dot(h, wu_ref[...].astype(BF16), preferred_element_type=F32)
    a = (g * jax.nn.sigmoid(g) * u).astype(BF16)
    o_ref[...] += jnp.dot(a, wd_ref[...].astype(BF16), preferred_element_type=F32)

    @pl.when(j == pl.num_programs(1) - 1)
    def _():
        gate = mod_ref[3 * sub + 2:3 * sub + 3, :]
        o_ref[...] = x_ref[...] + (0.5 * gate) * o_ref[...]


def _ffn(x, mod, gain, w_gu, w_down, *, layer, sub, tm, tf):
    s, d = x.shape
    dff = w_down.shape[1]
    nf = dff // tf
    return pl.pallas_call(
        functools.partial(_ffn_kernel, sub=sub),
        grid=(s // tm, nf),
        in_specs=[
            pl.BlockSpec((tm, d), lambda i, j: (i, 0), pipeline_mode=pl.Buffered(1)),
            pl.BlockSpec(mod.shape, lambda i, j: (0, 0)),
            pl.BlockSpec((1, d), lambda i, j: (0, 0)),
            pl.BlockSpec((None, d, tf), lambda i, j: (layer, 0, j)),
            pl.BlockSpec((None, d, tf), lambda i, j: (layer, 0, j + nf)),
            pl.BlockSpec((None, tf, d), lambda i, j: (layer, j, 0)),
        ],
        out_specs=pl.BlockSpec((tm, d), lambda i, j: (i, 0)),
        out_shape=jax.ShapeDtypeStruct((s, d), F32),
        scratch_shapes=[pltpu.VMEM((tm, d), BF16)],
        compiler_params=pltpu.CompilerParams(
            dimension_semantics=("parallel", "arbitrary"), vmem_limit_bytes=VMEM_LIMIT),
        name=f"ffn{sub}",
    )(x, mod, gain, w_gu, w_gu, w_down)


def _inproj_kernel(x_ref, mod_ref, g_ref, w_ref, o_ref, h_ref, *, sub):
    @pl.when(pl.program_id(1) == 0)
    def _():
        h = _norm_modulate(x_ref[...], g_ref[...],
                           mod_ref[3 * sub:3 * sub + 1, :],
                           mod_ref[3 * sub + 1:3 * sub + 2, :])
        h_ref[...] = h.astype(BF16)

    o_ref[...] = jnp.dot(h_ref[...], w_ref[...], preferred_element_type=F32)


def _inproj(x, mod, gain, w, *, sub, tm, tn):
    s, d = x.shape
    n = w.shape[1]
    return pl.pallas_call(
        functools.partial(_inproj_kernel, sub=sub),
        grid=(s // tm, n // tn),
        in_specs=[
            pl.BlockSpec((tm, d), lambda i, j: (i, 0)),
            pl.BlockSpec(mod.shape, lambda i, j: (0, 0)),
            pl.BlockSpec((1, d), lambda i, j: (0, 0)),
            pl.BlockSpec((d, tn), lambda i, j: (0, j)),
        ],
        out_specs=pl.BlockSpec((tm, tn), lambda i, j: (i, j)),
        out_shape=jax.ShapeDtypeStruct((s, n), F32),
        scratch_shapes=[pltpu.VMEM((tm, d), BF16)],
        compiler_params=pltpu.CompilerParams(
            dimension_semantics=("parallel", "arbitrary"), vmem_limit_bytes=VMEM_LIMIT),
        name="in_proj",
    )(x, mod, gain, w)


def _rms_over_rows(t):
    ms = jnp.mean(t * t, axis=1, keepdims=True)
    return t * lax.rsqrt(ms + EPS)


def _attn_kernel(q_ref, kv_ref, kvp_ref, qg_ref, kg_ref, bucket_ref, rel_ref, sink_ref,
                 o_ref, bias_scr, sink_scr, *, nblk):
    i = pl.program_id(0)

    from_prev = (lax.broadcasted_iota(jnp.int32, (BLK, BLK), 0)
                 > lax.broadcasted_iota(jnp.int32, (BLK, BLK), 1))

    @pl.when(i == 0)
    def _():
        bucket = bucket_ref[...]
        for h in range(N_HEADS):
            kh, g = divmod(h, GQA_GROUP)
            lanes = slice(g * BLK, (g + 1) * BLK)
            acc = jnp.zeros((BLK, BLK), F32)
            for b in range(NUM_BUCKETS):
                acc = jnp.where(bucket == b, rel_ref[b, h] * LOG2E, acc)
            bias_scr[kh, :, lanes] = acc
            bias_scr[N_KV_HEADS + kh, :, lanes] = jnp.where(from_prev, NEG_INF, acc)
            sink_scr[kh, :, lanes] = jnp.full((1, BLK), sink_ref[h] * LOG2E, F32)

    t = (nblk + 1) * BLK
    kv_t = jnp.concatenate([kvp_ref[...], kv_ref[...]], axis=0).T
    k_t = kv_t[0:N_KV_HEADS * HEAD_DIM].reshape(N_KV_HEADS, HEAD_DIM, t)
    k_gain = kg_ref[...] * qg_ref[...] * (HEAD_DIM ** -0.5 * LOG2E)
    kn = (_rms_over_rows(k_t) * k_gain).reshape(N_KV_HEADS * HEAD_DIM, t).T
    lane = lax.broadcasted_iota(jnp.int32, kn.shape, 1)
    k_sel = [jnp.where((lane // HEAD_DIM) == kh, kn, 0.0).astype(BF16)
             for kh in range(N_KV_HEADS)]
    v_t = [kv_t[(N_KV_HEADS + kh) * HEAD_DIM:(N_KV_HEADS + kh + 1) * HEAD_DIM].astype(BF16)
           for kh in range(N_KV_HEADS)]

    for b in range(nblk):
        rows = slice(b * BLK, (b + 1) * BLK)
        band = slice(b * BLK, (b + 2) * BLK)
        q_t = q_ref[rows, :].T.reshape(N_HEADS, HEAD_DIM, BLK)
        qn = _rms_over_rows(q_t).astype(BF16)
        out_t = []
        for kh in range(N_KV_HEADS):
            q_grp = jnp.concatenate([qn[kh * GQA_GROUP + g] for g in range(GQA_GROUP)],
                                    axis=1)
            q_dup = jnp.concatenate([q_grp] * N_KV_HEADS, axis=0)
            s = jnp.dot(k_sel[kh][band], q_dup, preferred_element_type=F32)
            table = kh + N_KV_HEADS * (i == 0).astype(jnp.int32) if b == 0 else kh
            e_parts, denom_parts = [], []
            for g in range(GQA_GROUP):
                lanes = slice(g * BLK, (g + 1) * BLK)
                sh = jnp.where(from_prev, s[0:BLK, lanes], s[BLK:2 * BLK, lanes])
                sh = sh + bias_scr[table, :, lanes]
                sink = sink_scr[kh, :, lanes]
                m = jnp.maximum(jnp.max(sh, axis=0, keepdims=True), sink)
                e = jnp.exp2(sh - m)
                denom_parts.append(jnp.sum(e, axis=0, keepdims=True) + jnp.exp2(sink - m))
                e_parts.append(jnp.concatenate(
                    [jnp.where(from_prev, e, 0.0), jnp.where(from_prev, 0.0, e)],
                    axis=0).astype(BF16))
            e = jnp.concatenate(e_parts, axis=1)
            denom = jnp.concatenate(denom_parts, axis=1)
            o = jnp.dot(v_t[kh][:, band], e, preferred_element_type=F32)
            o = o * (1.0 / denom)
            out_t += [o[:, g * BLK:(g + 1) * BLK] for g in range(GQA_GROUP)]
        o_ref[rows, :] = jnp.concatenate(out_t, axis=0).T.astype(BF16)


def _attention(z, q_gain, k_gain, rel_bias, sinks, *, tq):
    s = z.shape[0]
    nblk = tq // BLK
    gw = GQA_GROUP * BLK
    q_col = POOL_W // ATTN_W
    kv_col = (POOL_W + ATTN_W) // KV_W
    bucket_t = _rel_bucket_band().T
    jm, ql = np.meshgrid(np.arange(BLK), np.arange(BLK), indexing="ij")
    bucket_fold = jnp.asarray(np.where(jm > ql, bucket_t[:BLK], bucket_t[BLK:]))
    return pl.pallas_call(
        functools.partial(_attn_kernel, nblk=nblk),
        grid=(s // tq,),
        in_specs=[
            pl.BlockSpec((tq, ATTN_W), lambda i: (i, q_col)),
            pl.BlockSpec((tq, KV_W), lambda i: (i, kv_col)),
            pl.BlockSpec((BLK, KV_W), lambda i: (jnp.maximum(i * nblk - 1, 0), kv_col)),
            pl.BlockSpec((HEAD_DIM, 1), lambda i: (0, 0)),
            pl.BlockSpec((HEAD_DIM, 1), lambda i: (0, 0)),
            pl.BlockSpec((BLK, BLK), lambda i: (0, 0)),
            pl.BlockSpec(memory_space=pltpu.SMEM),
            pl.BlockSpec(memory_space=pltpu.SMEM),
        ],
        out_specs=pl.BlockSpec((tq, ATTN_W), lambda i: (i, 0)),
        out_shape=jax.ShapeDtypeStruct((s, ATTN_W), BF16),
        scratch_shapes=[
            pltpu.VMEM((2 * N_KV_HEADS, BLK, gw), F32),
            pltpu.VMEM((N_KV_HEADS, 1, gw), F32),
        ],
        compiler_params=pltpu.CompilerParams(
            dimension_semantics=("arbitrary",), vmem_limit_bytes=VMEM_LIMIT),
        name="swa_attention",
    )(z, z, z, q_gain.reshape(HEAD_DIM, 1), k_gain.reshape(HEAD_DIM, 1), bucket_fold,
      rel_bias, sinks)


def _mix_kernel(u_ref, up_ref, ga_ref, gb_ref, attn_ref, x_ref, mod_ref, pmix_ref,
                pscale_ref, wpool_ref, wattn_ref, wo_ref, o_ref, ubuf, *, sub, tm):
    i = pl.program_id(0)
    halo = up_ref[...]
    ubuf[0:POOL_HALO, :] = jnp.where(i == 0, 0.0, halo)
    ubuf[POOL_HALO:, :] = u_ref[...]

    t1 = i * tm + lax.broadcasted_iota(jnp.int32, (tm, POOL_GROUP_W), 0) + 1
    mixed = []
    for gi, w in enumerate(POOL_WINDOWS):
        cols = slice(gi * POOL_GROUP_W, (gi + 1) * POOL_GROUP_W)
        cur = ubuf[POOL_HALO:POOL_HALO + tm, cols]
        win = cur
        for dlt in range(1, w):
            win = win + ubuf[POOL_HALO - dlt:POOL_HALO - dlt + tm, cols]
        cnt = jnp.minimum(t1, w).astype(F32)
        pooled = (win / cnt - cur).astype(BF16)
        mg = jnp.dot(pooled, pmix_ref[gi], preferred_element_type=F32)
        mixed.append((mg * pscale_ref[:, cols]).astype(BF16))
    mixed = jnp.concatenate(mixed, axis=1)

    y_pool = jnp.dot(mixed, wpool_ref[...], preferred_element_type=F32)
    y_attn = jnp.dot(attn_ref[...], wattn_ref[...], preferred_element_type=F32)
    merged = jax.nn.sigmoid(ga_ref[...]) * y_pool + jax.nn.sigmoid(gb_ref[...]) * y_attn
    y = jnp.dot(merged.astype(BF16), wo_ref[...], preferred_element_type=F32)
    gate = mod_ref[3 * sub + 2:3 * sub + 3, :]
    o_ref[...] = x_ref[...] + gate * y


def _mix(z, attn, x, mod, pool_mix, pool_scale, w_pool_up, w_attn_up, w_o, *, sub, tm):
    s, d = x.shape
    const = dict(pipeline_mode=pl.Buffered(1))
    ga_off = POOL_W + ATTN_W + KV_W
    return pl.pallas_call(
        functools.partial(_mix_kernel, sub=sub, tm=tm),
        grid=(s // tm,),
        in_specs=[
            pl.BlockSpec((tm, POOL_W), lambda i: (i, 0)),
            pl.BlockSpec((POOL_HALO, POOL_W),
                         lambda i: (jnp.maximum(i * (tm // POOL_HALO) - 1, 0), 0)),
            pl.BlockSpec((pl.Element(tm), pl.Element(d)), lambda i: (i * tm, ga_off)),
            pl.BlockSpec((pl.Element(tm), pl.Element(d)), lambda i: (i * tm, ga_off + d)),
            pl.BlockSpec((tm, ATTN_W), lambda i: (i, 0)),
            pl.BlockSpec((tm, d), lambda i: (i, 0)),
            pl.BlockSpec(mod.shape, lambda i: (0, 0), **const),
            pl.BlockSpec(pool_mix.shape, lambda i: (0, 0, 0), **const),
            pl.BlockSpec((1, POOL_W), lambda i: (0, 0), **const),
            pl.BlockSpec(w_pool_up.shape, lambda i: (0, 0), **const),
            pl.BlockSpec(w_attn_up.shape, lambda i: (0, 0), **const),
            pl.BlockSpec(w_o.shape, lambda i: (0, 0), **const),
        ],
        out_specs=pl.BlockSpec((tm, d), lambda i: (i, 0)),
        out_shape=jax.ShapeDtypeStruct((s, d), F32),
        scratch_shapes=[pltpu.VMEM((tm + POOL_HALO, POOL_W), F32)],
        compiler_params=pltpu.CompilerParams(
            dimension_semantics=("arbitrary",), vmem_limit_bytes=VMEM_LIMIT),
        name="pool_merge_out",
    )(z, z, z, z, attn, x, mod, pool_mix, pool_scale, w_pool_up, w_attn_up, w_o)


def kernel(x, c, w_ada, b_ada, g_ffn1, w_ffn1_gu, w_ffn1_down, g_mix, w_in, pool_mix,
           pool_scale, w_pool_up, q_gain, k_gain, sinks, rel_bias, w_attn_up, w_o,
           g_ffn2, w_ffn2_gu, w_ffn2_down):
    b, s, d = x.shape
    depth = w_ada.shape[0]
    assert b == 1, "adaLN matvec and row tiling assume a single sequence"
    xs = x[0]
    for l in range(depth):
        mod = _ada(c.reshape(d, 1), w_ada[l:l + 1], b_ada[l:l + 1], tn=1024)
        mod = mod.reshape(3 * N_SUB, d)

        xs = _ffn(xs, mod, g_ffn1[l:l + 1], w_ffn1_gu, w_ffn1_down,
                  layer=l, sub=0, tm=1024, tf=256)

        z = _inproj(xs, mod, g_mix[l:l + 1], w_in[l].astype(BF16), sub=1, tm=1024, tn=1280)
        attn = _attention(z, q_gain[l], k_gain[l], rel_bias, sinks[l], tq=512)
        xs = _mix(z, attn, xs, mod, pool_mix[l].astype(BF16), pool_scale[l:l + 1],
                  w_pool_up[l].astype(BF16), w_attn_up[l].astype(BF16),
                  w_o[l].astype(BF16), sub=1, tm=256)

        xs = _ffn(xs, mod, g_ffn2[l:l + 1], w_ffn2_gu, w_ffn2_down,
                  layer=l, sub=2, tm=1024, tf=256)
    return xs[None]
```

```python
import functools

import jax
import jax.numpy as jnp
import numpy as np
from jax import lax
from jax.experimental import pallas as pl
from jax.experimental.pallas import tpu as pltpu

F32 = jnp.float32
BF16 = jnp.bfloat16

N_SUB = 3
POOL_WINDOWS = (2, 4, 8, 16)
POOL_GROUP_W = 256
POOL_W = len(POOL_WINDOWS) * POOL_GROUP_W
HEAD_DIM = 64
N_HEADS = 16
N_KV_HEADS = 2
GQA_GROUP = N_HEADS // N_KV_HEADS
ATTN_W = N_HEADS * HEAD_DIM
KV_W = 2 * N_KV_HEADS * HEAD_DIM
BLK = 128
NUM_BUCKETS = 32
MAX_EXACT = NUM_BUCKETS // 2
REL_MAX_DIST = 128
EPS = 1e-6
NEG_INF = -1e30
LOG2E = float(np.log2(np.e))
POOL_HALO = 16

V7X_VMEM_BYTES = 64 * 1024 * 1024
VMEM_LIMIT = V7X_VMEM_BYTES - 8 * 1024 * 1024


def _rel_bucket_band():
    ql = np.arange(BLK)[:, None]
    j = np.arange(2 * BLK)[None, :]
    n = np.clip(BLK + ql - j, 0, None)
    nf = np.maximum(n, 1).astype(np.float32)
    large = MAX_EXACT + (np.log(nf / MAX_EXACT) / np.log(REL_MAX_DIST / MAX_EXACT)
                         * (NUM_BUCKETS - MAX_EXACT)).astype(np.int32)
    large = np.minimum(large, NUM_BUCKETS - 1)
    return np.where(n < MAX_EXACT, n, large).astype(np.int32)


def _norm_modulate(x, gain, shift, scale):
    ms = jnp.mean(x * x, axis=-1, keepdims=True)
    y = x * lax.rsqrt(ms + EPS) * gain
    return y * (1.0 + scale) + shift


def _ada_kernel(c_ref, w_ref, b_ref, o_ref):
    cc = c_ref[...]
    sc = cc * jax.nn.sigmoid(cc)
    o_ref[...] = jnp.sum(w_ref[...] * sc, axis=0, keepdims=True) + b_ref[...]


def _ada(c_col, w_ada, b_ada, *, tn):
    d, n = w_ada.shape[1], w_ada.shape[2]
    return pl.pallas_call(
        _ada_kernel,
        grid=(n // tn,),
        in_specs=[
            pl.BlockSpec((d, 1), lambda j: (0, 0)),
            pl.BlockSpec((None, d, tn), lambda j: (0, 0, j)),
            pl.BlockSpec((1, tn), lambda j: (0, j)),
        ],
        out_specs=pl.BlockSpec((1, tn), lambda j: (0, j)),
        out_shape=jax.ShapeDtypeStruct((1, n), F32),
        compiler_params=pltpu.CompilerParams(
            dimension_semantics=("arbitrary",), vmem_limit_bytes=VMEM_LIMIT),
        name="ada_ln",
    )(c_col, w_ada, b_ada)


def _ffn_kernel(x_ref, mod_ref, g_ref, wg_ref, wu_ref, wd_ref, o_ref, h_ref, *, sub):
    j = pl.program_id(1)

    @pl.when(j == 0)
    def _():
        h = _norm_modulate(x_ref[...], g_ref[...],
                           mod_ref[3 * sub:3 * sub + 1, :],
                           mod_ref[3 * sub + 1:3 * sub + 2, :])
        h_ref[...] = h.astype(BF16)
        o_ref[...] = jnp.zeros_like(o_ref)

    h = h_ref[...]
    g = jnp.dot(h, wg_ref[...].astype(BF16), preferred_element_type=F32)
    u = jnp.dot(h, wu_ref[...].astype(BF16), preferred_element_type=F32)
    a = (g * jax.nn.sigmoid(g) * u).astype(BF16)
    o_ref[...] += jnp.dot(a, wd_ref[...].astype(BF16), preferred_element_type=F32)

    @pl.when(j == pl.num_programs(1) - 1)
    def _():
        gate = mod_ref[3 * sub + 2:3 * sub + 3, :]
        o_ref[...] = x_ref[...] + (0.5 * gate) * o_ref[...]


def _ffn(x, mod, gain, w_gu, w_down, *, layer, sub, tm, tf):
    s, d = x.shape
    dff = w_down.shape[1]
    nf = dff // tf
    return pl.pallas_call(
        functools.partial(_ffn_kernel, sub=sub),
        grid=(s // tm, nf),
        in_specs=[
            pl.BlockSpec((tm, d), lambda i, j: (i, 0), pipeline_mode=pl.Buffered(1)),
            pl.BlockSpec(mod.shape, lambda i, j: (0, 0)),
            pl.BlockSpec((1, d), lambda i, j: (0, 0)),
            pl.BlockSpec((None, d, tf), lambda i, j: (layer, 0, j)),
            pl.BlockSpec((None, d, tf), lambda i, j: (layer, 0, j + nf)),
            pl.BlockSpec((None, tf, d), lambda i, j: (layer, j, 0)),
        ],
        out_specs=pl.BlockSpec((tm, d), lambda i, j: (i, 0)),
        out_shape=jax.ShapeDtypeStruct((s, d), F32),
        scratch_shapes=[pltpu.VMEM((tm, d), BF16)],
        compiler_params=pltpu.CompilerParams(
            dimension_semantics=("parallel", "arbitrary"), vmem_limit_bytes=VMEM_LIMIT),
        name=f"ffn{sub}",
    )(x, mod, gain, w_gu, w_gu, w_down)


def _inproj_kernel(x_ref, mod_ref, g_ref, w_ref, o_ref, h_ref, *, sub):
    @pl.when(pl.program_id(1) == 0)
    def _():
        h = _norm_modulate(x_ref[...], g_ref[...],
                           mod_ref[3 * sub:3 * sub + 1, :],
                           mod_ref[3 * sub + 1:3 * sub + 2, :])
        h_ref[...] = h.astype(BF16)

    o_ref[...] = jnp.dot(h_ref[...], w_ref[...], preferred_element_type=F32)


def _inproj(x, mod, gain, w, *, sub, tm, tn):
    s, d = x.shape
    n = w.shape[1]
    return pl.pallas_call(
        functools.partial(_inproj_kernel, sub=sub),
        grid=(s // tm, n // tn),
        in_specs=[
            pl.BlockSpec((tm, d), lambda i, j: (i, 0)),
            pl.BlockSpec(mod.shape, lambda i, j: (0, 0)),
            pl.BlockSpec((1, d), lambda i, j: (0, 0)),
            pl.BlockSpec((d, tn), lambda i, j: (0, j)),
        ],
        out_specs=pl.BlockSpec((tm, tn), lambda i, j: (i, j)),
        out_shape=jax.ShapeDtypeStruct((s, n), F32),
        scratch_shapes=[pltpu.VMEM((tm, d), BF16)],
        compiler_params=pltpu.CompilerParams(
            dimension_semantics=("parallel", "arbitrary"), vmem_limit_bytes=VMEM_LIMIT),
        name="in_proj",
    )(x, mod, gain, w)


def _rms_over_rows(t):
    ms = jnp.mean(t * t, axis=1, keepdims=True)
    return t * lax.rsqrt(ms + EPS)


def _attn_kernel(q_ref, kv_ref, kvp_ref, qg_ref, kg_ref, bucket_ref, rel_ref, sink_ref,
                 o_ref, bias_scr, sink_scr, *, nblk):
    i = pl.program_id(0)

    from_prev = (lax.broadcasted_iota(jnp.int32, (BLK, BLK), 0)
                 > lax.broadcasted_iota(jnp.int32, (BLK, BLK), 1))

    @pl.when(i == 0)
    def _():
        bucket = bucket_ref[...]
        for h in range(N_HEADS):
            kh, g = divmod(h, GQA_GROUP)
            lanes = slice(g * BLK, (g + 1) * BLK)
            acc = jnp.zeros((BLK, BLK), F32)
            for b in range(NUM_BUCKETS):
                acc = jnp.where(bucket == b, rel_ref[b, h] * LOG2E, acc)
            bias_scr[kh, :, lanes] = acc
            bias_scr[N_KV_HEADS + kh, :, lanes] = jnp.where(from_prev, NEG_INF, acc)
            sink_scr[kh, :, lanes] = jnp.full((1, BLK), sink_ref[h] * LOG2E, F32)

    t = (nblk + 1) * BLK
    kv_t = jnp.concatenate([kvp_ref[...], kv_ref[...]], axis=0).T
    k_t = kv_t[0:N_KV_HEADS * HEAD_DIM].reshape(N_KV_HEADS, HEAD_DIM, t)
    k_gain = kg_ref[...] * qg_ref[...] * (HEAD_DIM ** -0.5 * LOG2E)
    kn = (_rms_over_rows(k_t) * k_gain).reshape(N_KV_HEADS * HEAD_DIM, t).T
    lane = lax.broadcasted_iota(jnp.int32, kn.shape, 1)
    k_sel = [jnp.where((lane // HEAD_DIM) == kh, kn, 0.0).astype(BF16)
             for kh in range(N_KV_HEADS)]
    v_t = [kv_t[(N_KV_HEADS + kh) * HEAD_DIM:(N_KV_HEADS + kh + 1) * HEAD_DIM].astype(BF16)
           for kh in range(N_KV_HEADS)]

    for b in range(nblk):
        rows = slice(b * BLK, (b + 1) * BLK)
        band = slice(b * BLK, (b + 2) * BLK)
        q_t = q_ref[rows, :].T.reshape(N_HEADS, HEAD_DIM, BLK)
        qn = _rms_over_rows(q_t).astype(BF16)
        out_t = []
        for kh in range(N_KV_HEADS):
            q_grp = jnp.concatenate([qn[kh * GQA_GROUP + g] for g in range(GQA_GROUP)],
                                    axis=1)
            q_dup = jnp.concatenate([q_grp] * N_KV_HEADS, axis=0)
            s = jnp.dot(k_sel[kh][band], q_dup, preferred_element_type=F32)
            table = kh + N_KV_HEADS * (i == 0).astype(jnp.int32) if b == 0 else kh
            e_parts, denom_parts = [], []
            for g in range(GQA_GROUP):
                lanes = slice(g * BLK, (g + 1) * BLK)
                sh = jnp.where(from_prev, s[0:BLK, lanes], s[BLK:2 * BLK, lanes])
                sh = sh + bias_scr[table, :, lanes]
                sink = sink_scr[kh, :, lanes]
                m = jnp.maximum(jnp.max(sh, axis=0, keepdims=True), sink)
                e = jnp.exp2(sh - m)
                denom_parts.append(jnp.sum(e, axis=0, keepdims=True) + jnp.exp2(sink - m))
                e_parts.append(jnp.concatenate(
                    [jnp.where(from_prev, e, 0.0), jnp.where(from_prev, 0.0, e)],
                    axis=0).astype(BF16))
            e = jnp.concatenate(e_parts, axis=1)
            denom = jnp.concatenate(denom_parts, axis=1)
            o = jnp.dot(v_t[kh][:, band], e, preferred_element_type=F32)
            o = o * (1.0 / denom)
            out_t += [o[:, g * BLK:(g + 1) * BLK] for g in range(GQA_GROUP)]
        o_ref[rows, :] = jnp.concatenate(out_t, axis=0).T.astype(BF16)


def _attention(z, q_gain, k_gain, rel_bias, sinks, *, tq):
    s = z.shape[0]
    nblk = tq // BLK
    gw = GQA_GROUP * BLK
    q_col = POOL_W // ATTN_W
    kv_col = (POOL_W + ATTN_W) // KV_W
    bucket_t = _rel_bucket_band().T
    jm, ql = np.meshgrid(np.arange(BLK), np.arange(BLK), indexing="ij")
    bucket_fold = jnp.asarray(np.where(jm > ql, bucket_t[:BLK], bucket_t[BLK:]))
    return pl.pallas_call(
        functools.partial(_attn_kernel, nblk=nblk),
        grid=(s // tq,),
        in_specs=[
            pl.BlockSpec((tq, ATTN_W), lambda i: (i, q_col)),
            pl.BlockSpec((tq, KV_W), lambda i: (i, kv_col)),
            pl.BlockSpec((BLK, KV_W), lambda i: (jnp.maximum(i * nblk - 1, 0), kv_col)),
            pl.BlockSpec((HEAD_DIM, 1), lambda i: (0, 0)),
            pl.BlockSpec((HEAD_DIM, 1), lambda i: (0, 0)),
            pl.BlockSpec((BLK, BLK), lambda i: (0, 0)),
            pl.BlockSpec(memory_space=pltpu.SMEM),
            pl.BlockSpec(memory_space=pltpu.SMEM),
        ],
        out_specs=pl.BlockSpec((tq, ATTN_W), lambda i: (i, 0)),
        out_shape=jax.ShapeDtypeStruct((s, ATTN_W), BF16),
        scratch_shapes=[
            pltpu.VMEM((2 * N_KV_HEADS, BLK, gw), F32),
            pltpu.VMEM((N_KV_HEADS, 1, gw), F32),
        ],
        compiler_params=pltpu.CompilerParams(
            dimension_semantics=("arbitrary",), vmem_limit_bytes=VMEM_LIMIT),
        name="swa_attention",
    )(z, z, z, q_gain.reshape(HEAD_DIM, 1), k_gain.reshape(HEAD_DIM, 1), bucket_fold,
      rel_bias, sinks)


def _mix_kernel(u_ref, up_ref, ga_ref, gb_ref, attn_ref, x_ref, mod_ref, pmix_ref,
                pscale_ref, wpool_ref, wattn_ref, wo_ref, o_ref, ubuf, *, sub, tm):
    i = pl.program_id(0)
    halo = up_ref[...]
    ubuf[0:POOL_HALO, :] = jnp.where(i == 0, 0.0, halo)
    ubuf[POOL_HALO:, :] = u_ref[...]

    t1 = i * tm + lax.broadcasted_iota(jnp.int32, (tm, POOL_GROUP_W), 0) + 1
    mixed = []
    for gi, w in enumerate(POOL_WINDOWS):
        cols = slice(gi * POOL_GROUP_W, (gi + 1) * POOL_GROUP_W)
        cur = ubuf[POOL_HALO:POOL_HALO + tm, cols]
        win = cur
        for dlt in range(1, w):
            win = win + ubuf[POOL_HALO - dlt:POOL_HALO - dlt + tm, cols]
        cnt = jnp.minimum(t1, w).astype(F32)
        pooled = (win / cnt - cur).astype(BF16)
        mg = jnp.dot(pooled, pmix_ref[gi], preferred_element_type=F32)
        mixed.append((mg * pscale_ref[:, cols]).astype(BF16))
    mixed = jnp.concatenate(mixed, axis=1)

    y_pool = jnp.dot(mixed, wpool_ref[...], preferred_element_type=F32)
    y_attn = jnp.dot(attn_ref[...], wattn_ref[...], preferred_element_type=F32)
    merged = jax.nn.sigmoid(ga_ref[...]) * y_pool + jax.nn.sigmoid(gb_ref[...]) * y_attn
    y = jnp.dot(merged.astype(BF16), wo_ref[...], preferred_element_type=F32)
    gate = mod_ref[3 * sub + 2:3 * sub + 3, :]
    o_ref[...] = x_ref[...] + gate * y


def _mix(z, attn, x, mod, pool_mix, pool_scale, w_pool_up, w_attn_up, w_o, *, sub, tm):
    s, d = x.shape
    const = dict(pipeline_mode=pl.Buffered(1))
    ga_off = POOL_W + ATTN_W + KV_W
    return pl.pallas_call(
        functools.partial(_mix_kernel, sub=sub, tm=tm),
        grid=(s // tm,),
        in_specs=[
            pl.BlockSpec((tm, POOL_W), lambda i: (i, 0)),
            pl.BlockSpec((POOL_HALO, POOL_W),
                         lambda i: (jnp.maximum(i * (tm // POOL_HALO) - 1, 0), 0)),
            pl.BlockSpec((pl.Element(tm), pl.Element(d)), lambda i: (i * tm, ga_off)),
            pl.BlockSpec((pl.Element(tm), pl.Element(d)), lambda i: (i * tm, ga_off + d)),
            pl.BlockSpec((tm, ATTN_W), lambda i: (i, 0)),
            pl.BlockSpec((tm, d), lambda i: (i, 0)),
            pl.BlockSpec(mod.shape, lambda i: (0, 0), **const),
            pl.BlockSpec(pool_mix.shape, lambda i: (0, 0, 0), **const),
            pl.BlockSpec((1, POOL_W), lambda i: (0, 0), **const),
            pl.BlockSpec(w_pool_up.shape, lambda i: (0, 0), **const),
            pl.BlockSpec(w_attn_up.shape, lambda i: (0, 0), **const),
            pl.BlockSpec(w_o.shape, lambda i: (0, 0), **const),
        ],
        out_specs=pl.BlockSpec((tm, d), lambda i: (i, 0)),
        out_shape=jax.ShapeDtypeStruct((s, d), F32),
        scratch_shapes=[pltpu.VMEM((tm + POOL_HALO, POOL_W), F32)],
        compiler_params=pltpu.CompilerParams(
            dimension_semantics=("arbitrary",), vmem_limit_bytes=VMEM_LIMIT),
        name="pool_merge_out",
    )(z, z, z, z, attn, x, mod, pool_mix, pool_scale, w_pool_up, w_attn_up, w_o)


def kernel(x, c, w_ada, b_ada, g_ffn1, w_ffn1_gu, w_ffn1_down, g_mix, w_in, pool_mix,
           pool_scale, w_pool_up, q_gain, k_gain, sinks, rel_bias, w_attn_up, w_o,
           g_ffn2, w_ffn2_gu, w_ffn2_down):
    b, s, d = x.shape
    depth = w_ada.shape[0]
    assert b == 1, "adaLN matvec and row tiling assume a single sequence"
    xs = x[0]
    for l in range(depth):
        mod = _ada(c.reshape(d, 1), w_ada[l:l + 1], b_ada[l:l + 1], tn=1024)
        mod = mod.reshape(3 * N_SUB, d)

        xs = _ffn(xs, mod, g_ffn1[l:l + 1], w_ffn1_gu.astype(BF16), w_ffn1_down.astype(BF16),
                  layer=l, sub=0, tm=1024, tf=512)

        z = _inproj(xs, mod, g_mix[l:l + 1], w_in[l].astype(BF16), sub=1, tm=1024, tn=1280)
        attn = _attention(z, q_gain[l], k_gain[l], rel_bias, sinks[l], tq=512)
        xs = _mix(z, attn, xs, mod, pool_mix[l].astype(BF16), pool_scale[l:l + 1],
                  w_pool_up[l].astype(BF16), w_attn_up[l].astype(BF16),
                  w_o[l].astype(BF16), sub=1, tm=256)

        xs = _ffn(xs, mod, g_ffn2[l:l + 1], w_ffn2_gu, w_ffn2_down,
                  layer=l, sub=2, tm=1024, tf=256)
    return xs[None]
```

```python
import functools

import jax
import jax.numpy as jnp
import numpy as np
from jax import lax
from jax.experimental import pallas as pl
from jax.experimental.pallas import tpu as pltpu

F32 = jnp.float32
BF16 = jnp.bfloat16

N_SUB = 3
POOL_WINDOWS = (2, 4, 8, 16)
POOL_GROUP_W = 256
POOL_W = len(POOL_WINDOWS) * POOL_GROUP_W
HEAD_DIM = 64
N_HEADS = 16
N_KV_HEADS = 2
GQA_GROUP = N_HEADS // N_KV_HEADS
ATTN_W = N_HEADS * HEAD_DIM
KV_W = 2 * N_KV_HEADS * HEAD_DIM
BLK = 128
NUM_BUCKETS = 32
MAX_EXACT = NUM_BUCKETS // 2
REL_MAX_DIST = 128
EPS = 1e-6
NEG_INF = -1e30
LOG2E = float(np.log2(np.e))
POOL_HALO = 16

LANES = 128
BF16_SUBLANES = 16

V7X_VMEM_BYTES = 64 * 1024 * 1024
VMEM_LIMIT = V7X_VMEM_BYTES - 8 * 1024 * 1024


def _rel_bucket_band():
    ql = np.arange(BLK)[:, None]
    j = np.arange(2 * BLK)[None, :]
    n = np.clip(BLK + ql - j, 0, None)
    nf = np.maximum(n, 1).astype(np.float32)
    large = MAX_EXACT + (np.log(nf / MAX_EXACT) / np.log(REL_MAX_DIST / MAX_EXACT)
                         * (NUM_BUCKETS - MAX_EXACT)).astype(np.int32)
    large = np.minimum(large, NUM_BUCKETS - 1)
    return np.where(n < MAX_EXACT, n, large).astype(np.int32)


def _norm_modulate(x, gain, shift, scale):
    ms = jnp.mean(x * x, axis=-1, keepdims=True)
    y = x * lax.rsqrt(ms + EPS) * gain
    return y * (1.0 + scale) + shift


def _ada_kernel(c_ref, w_ref, b_ref, o_ref):
    cc = c_ref[...]
    sc = cc * jax.nn.sigmoid(cc)
    o_ref[...] = jnp.sum(w_ref[...] * sc, axis=0, keepdims=True) + b_ref[...]


def _ada(c_col, w_ada, b_ada, *, tn):
    d, n = w_ada.shape[1], w_ada.shape[2]
    return pl.pallas_call(
        _ada_kernel,
        grid=(n // tn,),
        in_specs=[
            pl.BlockSpec((d, 1), lambda j: (0, 0)),
            pl.BlockSpec((None, d, tn), lambda j: (0, 0, j)),
            pl.BlockSpec((1, tn), lambda j: (0, j)),
        ],
        out_specs=pl.BlockSpec((1, tn), lambda j: (0, j)),
        out_shape=jax.ShapeDtypeStruct((1, n), F32),
        compiler_params=pltpu.CompilerParams(
            dimension_semantics=("arbitrary",), vmem_limit_bytes=VMEM_LIMIT),
        name="ada_ln",
    )(c_col, w_ada, b_ada)


def _ffn_kernel(x_ref, mod_ref, g_ref, wg_ref, wu_ref, wd_ref, *rest, sub, n_side):
    side_in, o_ref, side_out, h_ref = (rest[:n_side], rest[n_side],
                                       rest[n_side + 1:2 * n_side + 1], rest[-1])
    j = pl.program_id(1)

    @pl.when(j == 0)
    def _():
        h = _norm_modulate(x_ref[...], g_ref[...],
                           mod_ref[3 * sub:3 * sub + 1, :],
                           mod_ref[3 * sub + 1:3 * sub + 2, :])
        h_ref[...] = h.astype(BF16)
        o_ref[...] = jnp.zeros_like(o_ref)

    h = h_ref[...]
    g = jnp.dot(h, wg_ref[...].astype(BF16), preferred_element_type=F32)
    u = jnp.dot(h, wu_ref[...].astype(BF16), preferred_element_type=F32)
    a = (g * jax.nn.sigmoid(g) * u).astype(BF16)
    o_ref[...] += jnp.dot(a, wd_ref[...].astype(BF16), preferred_element_type=F32)

    for src, dst in zip(side_in, side_out):
        dst[...] = src[...].astype(BF16)

    @pl.when(j == pl.num_programs(1) - 1)
    def _():
        gate = mod_ref[3 * sub + 2:3 * sub + 3, :]
        o_ref[...] = x_ref[...] + (0.5 * gate) * o_ref[...]


def _side_cast_blocking(rows, cols, n_steps):
    best = None
    for bc in range(LANES, cols + 1, LANES):
        for br in range(BF16_SUBLANES, rows + 1, BF16_SUBLANES):
            if cols % bc or rows % br or (rows // br) * (cols // bc) > n_steps:
                continue
            if best is None or br * bc < best[0] * best[1]:
                best = (br, bc)
    assert best is not None, (rows, cols, n_steps)
    return best


def _ffn(x, mod, gain, w_gu, w_down, side=(), *, layer, sub, tm, tf, x_buffers):
    s, d = x.shape
    dff = w_down.shape[1]
    ni, nf = s // tm, dff // tf

    side_specs_in, side_specs_out, side_shapes = [], [], []
    for w in side:
        rows, cols = w.shape[1:]
        br, bc = _side_cast_blocking(rows, cols, ni * nf)
        ncb, last = cols // bc, (rows // br) * (cols // bc) - 1

        def block_index(i, j, ncb=ncb, last=last):
            t = jnp.minimum(i * nf + j, last)
            return t // ncb, t % ncb

        side_specs_in.append(pl.BlockSpec(
            (None, br, bc), lambda i, j, f=block_index: (layer, *f(i, j))))
        side_specs_out.append(pl.BlockSpec((br, bc), block_index))
        side_shapes.append(jax.ShapeDtypeStruct((rows, cols), BF16))

    x_mode = {} if x_buffers == 2 else dict(pipeline_mode=pl.Buffered(x_buffers))
    outs = pl.pallas_call(
        functools.partial(_ffn_kernel, sub=sub, n_side=len(side)),
        grid=(ni, nf),
        in_specs=[
            pl.BlockSpec((tm, d), lambda i, j: (i, 0), **x_mode),
            pl.BlockSpec(mod.shape, lambda i, j: (0, 0)),
            pl.BlockSpec((1, d), lambda i, j: (0, 0)),
            pl.BlockSpec((None, d, tf), lambda i, j: (layer, 0, j)),
            pl.BlockSpec((None, d, tf), lambda i, j: (layer, 0, j + nf)),
            pl.BlockSpec((None, tf, d), lambda i, j: (layer, j, 0)),
            *side_specs_in,
        ],
        out_specs=[pl.BlockSpec((tm, d), lambda i, j: (i, 0)), *side_specs_out],
        out_shape=[jax.ShapeDtypeStruct((s, d), F32), *side_shapes],
        scratch_shapes=[pltpu.VMEM((tm, d), BF16)],
        compiler_params=pltpu.CompilerParams(
            dimension_semantics=("arbitrary", "arbitrary"), vmem_limit_bytes=VMEM_LIMIT),
        name=f"ffn{sub}",
    )(x, mod, gain, w_gu, w_gu, w_down, *side)
    return outs[0], outs[1:]


def _inproj_kernel(x_ref, mod_ref, g_ref, w_ref, o_ref, h_ref, *, sub):
    @pl.when(pl.program_id(1) == 0)
    def _():
        h = _norm_modulate(x_ref[...], g_ref[...],
                           mod_ref[3 * sub:3 * sub + 1, :],
                           mod_ref[3 * sub + 1:3 * sub + 2, :])
        h_ref[...] = h.astype(BF16)

    o_ref[...] = jnp.dot(h_ref[...], w_ref[...], preferred_element_type=F32)


def _inproj(x, mod, gain, w, *, sub, tm, tn):
    s, d = x.shape
    n = w.shape[1]
    return pl.pallas_call(
        functools.partial(_inproj_kernel, sub=sub),
        grid=(s // tm, n // tn),
        in_specs=[
            pl.BlockSpec((tm, d), lambda i, j: (i, 0)),
            pl.BlockSpec(mod.shape, lambda i, j: (0, 0)),
            pl.BlockSpec((1, d), lambda i, j: (0, 0)),
            pl.BlockSpec((d, tn), lambda i, j: (0, j)),
        ],
        out_specs=pl.BlockSpec((tm, tn), lambda i, j: (i, j)),
        out_shape=jax.ShapeDtypeStruct((s, n), F32),
        scratch_shapes=[pltpu.VMEM((tm, d), BF16)],
        compiler_params=pltpu.CompilerParams(
            dimension_semantics=("parallel", "arbitrary"), vmem_limit_bytes=VMEM_LIMIT),
        name="in_proj",
    )(x, mod, gain, w)


def _rms_over_rows(t):
    ms = jnp.mean(t * t, axis=1, keepdims=True)
    return t * lax.rsqrt(ms + EPS)


def _attn_kernel(q_ref, kv_ref, kvp_ref, qg_ref, kg_ref, bucket_ref, rel_ref, sink_ref,
                 o_ref, bias_scr, sink_scr, *, nblk):
    i = pl.program_id(0)

    from_prev = (lax.broadcasted_iota(jnp.int32, (BLK, BLK), 0)
                 > lax.broadcasted_iota(jnp.int32, (BLK, BLK), 1))

    @pl.when(i == 0)
    def _():
        bucket = bucket_ref[...]
        for h in range(N_HEADS):
            kh, g = divmod(h, GQA_GROUP)
            lanes = slice(g * BLK, (g + 1) * BLK)
            acc = jnp.zeros((BLK, BLK), F32)
            for b in range(NUM_BUCKETS):
                acc = jnp.where(bucket == b, rel_ref[b, h] * LOG2E, acc)
            bias_scr[kh, :, lanes] = acc
            bias_scr[N_KV_HEADS + kh, :, lanes] = jnp.where(from_prev, NEG_INF, acc)
            sink_scr[kh, :, lanes] = jnp.full((1, BLK), sink_ref[h] * LOG2E, F32)

    t = (nblk + 1) * BLK
    kv_t = jnp.concatenate([kvp_ref[...], kv_ref[...]], axis=0).T
    k_t = kv_t[0:N_KV_HEADS * HEAD_DIM].reshape(N_KV_HEADS, HEAD_DIM, t)
    k_gain = kg_ref[...] * qg_ref[...] * (HEAD_DIM ** -0.5 * LOG2E)
    kn = (_rms_over_rows(k_t) * k_gain).reshape(N_KV_HEADS * HEAD_DIM, t).T
    lane = lax.broadcasted_iota(jnp.int32, kn.shape, 1)
    k_sel = [jnp.where((lane // HEAD_DIM) == kh, kn, 0.0).astype(BF16)
             for kh in range(N_KV_HEADS)]
    v_t = [kv_t[(N_KV_HEADS + kh) * HEAD_DIM:(N_KV_HEADS + kh + 1) * HEAD_DIM].astype(BF16)
           for kh in range(N_KV_HEADS)]

    for b in range(nblk):
        rows = slice(b * BLK, (b + 1) * BLK)
        band = slice(b * BLK, (b + 2) * BLK)
        q_t = q_ref[rows, :].T.reshape(N_HEADS, HEAD_DIM, BLK)
        qn = _rms_over_rows(q_t).astype(BF16)
        out_t = []
        for kh in range(N_KV_HEADS):
            q_grp = jnp.concatenate([qn[kh * GQA_GROUP + g] for g in range(GQA_GROUP)],
                                    axis=1)
            q_dup = jnp.concatenate([q_grp] * N_KV_HEADS, axis=0)
            s = jnp.dot(k_sel[kh][band], q_dup, preferred_element_type=F32)
            table = kh + N_KV_HEADS * (i == 0).astype(jnp.int32) if b == 0 else kh
            e_parts, denom_parts = [], []
            for g in range(GQA_GROUP):
                lanes = slice(g * BLK, (g + 1) * BLK)
                sh = jnp.where(from_prev, s[0:BLK, lanes], s[BLK:2 * BLK, lanes])
                sh = sh + bias_scr[table, :, lanes]
                sink = sink_scr[kh, :, lanes]
                m = jnp.maximum(jnp.max(sh, axis=0, keepdims=True), sink)
                e = jnp.exp2(sh - m)
                denom_parts.append(jnp.sum(e, axis=0, keepdims=True) + jnp.exp2(sink - m))
                e_parts.append(jnp.concatenate(
                    [jnp.where(from_prev, e, 0.0), jnp.where(from_prev, 0.0, e)],
                    axis=0).astype(BF16))
            e = jnp.concatenate(e_parts, axis=1)
            denom = jnp.concatenate(denom_parts, axis=1)
            o = jnp.dot(v_t[kh][:, band], e, preferred_element_type=F32)
            o = o * (1.0 / denom)
            out_t += [o[:, g * BLK:(g + 1) * BLK] for g in range(GQA_GROUP)]
        o_ref[rows, :] = jnp.concatenate(out_t, axis=0).T.astype(BF16)


def _attention(z, q_gain, k_gain, rel_bias, sinks, *, tq):
    s = z.shape[0]
    nblk = tq // BLK
    gw = GQA_GROUP * BLK
    q_col = POOL_W // ATTN_W
    kv_col = (POOL_W + ATTN_W) // KV_W
    bucket_t = _rel_bucket_band().T
    jm, ql = np.meshgrid(np.arange(BLK), np.arange(BLK), indexing="ij")
    bucket_fold = jnp.asarray(np.where(jm > ql, bucket_t[:BLK], bucket_t[BLK:]))
    return pl.pallas_call(
        functools.partial(_attn_kernel, nblk=nblk),
        grid=(s // tq,),
        in_specs=[
            pl.BlockSpec((tq, ATTN_W), lambda i: (i, q_col)),
            pl.BlockSpec((tq, KV_W), lambda i: (i, kv_col)),
            pl.BlockSpec((BLK, KV_W), lambda i: (jnp.maximum(i * nblk - 1, 0), kv_col)),
            pl.BlockSpec((HEAD_DIM, 1), lambda i: (0, 0)),
            pl.BlockSpec((HEAD_DIM, 1), lambda i: (0, 0)),
            pl.BlockSpec((BLK, BLK), lambda i: (0, 0)),
            pl.BlockSpec(memory_space=pltpu.SMEM),
            pl.BlockSpec(memory_space=pltpu.SMEM),
        ],
        out_specs=pl.BlockSpec((tq, ATTN_W), lambda i: (i, 0)),
        out_shape=jax.ShapeDtypeStruct((s, ATTN_W), BF16),
        scratch_shapes=[
            pltpu.VMEM((2 * N_KV_HEADS, BLK, gw), F32),
            pltpu.VMEM((N_KV_HEADS, 1, gw), F32),
        ],
        compiler_params=pltpu.CompilerParams(
            dimension_semantics=("arbitrary",), vmem_limit_bytes=VMEM_LIMIT),
        name="swa_attention",
    )(z, z, z, q_gain.reshape(HEAD_DIM, 1), k_gain.reshape(HEAD_DIM, 1), bucket_fold,
      rel_bias, sinks)


def _mix_kernel(u_ref, up_ref, ga_ref, gb_ref, attn_ref, x_ref, mod_ref, pmix_ref,
                pscale_ref, wpool_ref, wattn_ref, wo_ref, o_ref, ubuf, *, sub, tm):
    i = pl.program_id(0)
    halo = up_ref[...]
    ubuf[0:POOL_HALO, :] = jnp.where(i == 0, 0.0, halo)
    ubuf[POOL_HALO:, :] = u_ref[...]

    t1 = i * tm + lax.broadcasted_iota(jnp.int32, (tm, POOL_GROUP_W), 0) + 1
    mixed = []
    for gi, w in enumerate(POOL_WINDOWS):
        cols = slice(gi * POOL_GROUP_W, (gi + 1) * POOL_GROUP_W)
        cur = ubuf[POOL_HALO:POOL_HALO + tm, cols]
        win = cur
        for dlt in range(1, w):
            win = win + ubuf[POOL_HALO - dlt:POOL_HALO - dlt + tm, cols]
        cnt = jnp.minimum(t1, w).astype(F32)
        pooled = (win / cnt - cur).astype(BF16)
        mg = jnp.dot(pooled, pmix_ref[gi], preferred_element_type=F32)
        mixed.append((mg * pscale_ref[:, cols]).astype(BF16))
    mixed = jnp.concatenate(mixed, axis=1)

    y_pool = jnp.dot(mixed, wpool_ref[...], preferred_element_type=F32)
    y_attn = jnp.dot(attn_ref[...], wattn_ref[...], preferred_element_type=F32)
    merged = jax.nn.sigmoid(ga_ref[...]) * y_pool + jax.nn.sigmoid(gb_ref[...]) * y_attn
    y = jnp.dot(merged.astype(BF16), wo_ref[...], preferred_element_type=F32)
    gate = mod_ref[3 * sub + 2:3 * sub + 3, :]
    o_ref[...] = x_ref[...] + gate * y


def _mix(z, attn, x, mod, pool_mix, pool_scale, w_pool_up, w_attn_up, w_o, *, sub, tm):
    s, d = x.shape
    const = dict(pipeline_mode=pl.Buffered(1))
    ga_off = POOL_W + ATTN_W + KV_W
    return pl.pallas_call(
        functools.partial(_mix_kernel, sub=sub, tm=tm),
        grid=(s // tm,),
        in_specs=[
            pl.BlockSpec((tm, POOL_W), lambda i: (i, 0)),
            pl.BlockSpec((POOL_HALO, POOL_W),
                         lambda i: (jnp.maximum(i * (tm // POOL_HALO) - 1, 0), 0)),
            pl.BlockSpec((pl.Element(tm), pl.Element(d)), lambda i: (i * tm, ga_off)),
            pl.BlockSpec((pl.Element(tm), pl.Element(d)), lambda i: (i * tm, ga_off + d)),
            pl.BlockSpec((tm, ATTN_W), lambda i: (i, 0)),
            pl.BlockSpec((tm, d), lambda i: (i, 0)),
            pl.BlockSpec(mod.shape, lambda i: (0, 0), **const),
            pl.BlockSpec(pool_mix.shape, lambda i: (0, 0, 0), **const),
            pl.BlockSpec((1, POOL_W), lambda i: (0, 0), **const),
            pl.BlockSpec(w_pool_up.shape, lambda i: (0, 0), **const),
            pl.BlockSpec(w_attn_up.shape, lambda i: (0, 0), **const),
            pl.BlockSpec(w_o.shape, lambda i: (0, 0), **const),
        ],
        out_specs=pl.BlockSpec((tm, d), lambda i: (i, 0)),
        out_shape=jax.ShapeDtypeStruct((s, d), F32),
        scratch_shapes=[pltpu.VMEM((tm + POOL_HALO, POOL_W), F32)],
        compiler_params=pltpu.CompilerParams(
            dimension_semantics=("arbitrary",), vmem_limit_bytes=VMEM_LIMIT),
        name="pool_merge_out",
    )(z, z, z, z, attn, x, mod, pool_mix, pool_scale, w_pool_up, w_attn_up, w_o)


def kernel(x, c, w_ada, b_ada, g_ffn1, w_ffn1_gu, w_ffn1_down, g_mix, w_in, pool_mix,
           pool_scale, w_pool_up, q_gain, k_gain, sinks, rel_bias, w_attn_up, w_o,
           g_ffn2, w_ffn2_gu, w_ffn2_down):
    b, s, d = x.shape
    depth = w_ada.shape[0]
    assert b == 1, "adaLN matvec and row tiling assume a single sequence"
    xs = x[0]
    for l in range(depth):
        mod = _ada(c.reshape(d, 1), w_ada[l:l + 1], b_ada[l:l + 1], tn=1024)
        mod = mod.reshape(3 * N_SUB, d)

        later = (w_ffn2_gu, w_ffn2_down, w_in, pool_mix.reshape(depth, POOL_W, POOL_GROUP_W),
                 w_pool_up, w_attn_up, w_o)
        xs, (gu2, down2, w_in_b, pmix_b, wpool_b, wattn_b, wo_b) = _ffn(
            xs, mod, g_ffn1[l:l + 1], w_ffn1_gu, w_ffn1_down, later,
            layer=l, sub=0, tm=1024, tf=256, x_buffers=1)

        z = _inproj(xs, mod, g_mix[l:l + 1], w_in_b, sub=1, tm=1024, tn=1280)
        attn = _attention(z, q_gain[l], k_gain[l], rel_bias, sinks[l], tq=512)
        xs = _mix(z, attn, xs, mod, pmix_b.reshape(pool_mix.shape[1:]), pool_scale[l:l + 1],
                  wpool_b, wattn_b, wo_b, sub=1, tm=256)

        xs, _ = _ffn(xs, mod, g_ffn2[l:l + 1], gu2[None], down2[None],
                     layer=0, sub=2, tm=512, tf=512, x_buffers=2)
    return xs[None]
```

```python
import functools

import jax
import jax.numpy as jnp
import numpy as np
from jax import lax
from jax.experimental import pallas as pl
from jax.experimental.pallas import tpu as pltpu

F32 = jnp.float32
BF16 = jnp.bfloat16

N_SUB = 3
POOL_WINDOWS = (2, 4, 8, 16)
POOL_GROUP_W = 256
POOL_W = len(POOL_WINDOWS) * POOL_GROUP_W
HEAD_DIM = 64
N_HEADS = 16
N_KV_HEADS = 2
GQA_GROUP = N_HEADS // N_KV_HEADS
ATTN_W = N_HEADS * HEAD_DIM
KV_W = 2 * N_KV_HEADS * HEAD_DIM
BLK = 128
NUM_BUCKETS = 32
MAX_EXACT = NUM_BUCKETS // 2
REL_MAX_DIST = 128
EPS = 1e-6
NEG_INF = -1e30
LOG2E = float(np.log2(np.e))
POOL_HALO = 16

LANES = 128
BF16_SUBLANES = 16
MXU_COLS = 256

V7X_VMEM_BYTES = 64 * 1024 * 1024
VMEM_LIMIT = V7X_VMEM_BYTES - 8 * 1024 * 1024


def _rel_bucket_band():
    ql = np.arange(BLK)[:, None]
    j = np.arange(2 * BLK)[None, :]
    n = np.clip(BLK + ql - j, 0, None)
    nf = np.maximum(n, 1).astype(np.float32)
    large = MAX_EXACT + (np.log(nf / MAX_EXACT) / np.log(REL_MAX_DIST / MAX_EXACT)
                         * (NUM_BUCKETS - MAX_EXACT)).astype(np.int32)
    large = np.minimum(large, NUM_BUCKETS - 1)
    return np.where(n < MAX_EXACT, n, large).astype(np.int32)


def _norm_modulate(x, gain, shift, scale):
    ms = jnp.mean(x * x, axis=-1, keepdims=True)
    y = x * lax.rsqrt(ms + EPS) * gain
    return y * (1.0 + scale) + shift


def _ada_kernel(c_ref, w_ref, b_ref, o_ref):
    cc = c_ref[...]
    sc = cc * jax.nn.sigmoid(cc)
    o_ref[...] = jnp.sum(w_ref[...] * sc, axis=0, keepdims=True) + b_ref[...]


def _ada(c_col, w_ada, b_ada, *, tn):
    d, n = w_ada.shape[1], w_ada.shape[2]
    return pl.pallas_call(
        _ada_kernel,
        grid=(n // tn,),
        in_specs=[
            pl.BlockSpec((d, 1), lambda j: (0, 0)),
            pl.BlockSpec((None, d, tn), lambda j: (0, 0, j)),
            pl.BlockSpec((1, tn), lambda j: (0, j)),
        ],
        out_specs=pl.BlockSpec((1, tn), lambda j: (0, j)),
        out_shape=jax.ShapeDtypeStruct((1, n), F32),
        compiler_params=pltpu.CompilerParams(
            dimension_semantics=("arbitrary",), vmem_limit_bytes=VMEM_LIMIT),
        name="ada_ln",
    )(c_col, w_ada, b_ada)


def _ffn_kernel(x_ref, mod_ref, g_ref, wg_ref, wu_ref, wd_ref, *rest, sub, n_side):
    side_in, o_ref, side_out, h_ref = (rest[:n_side], rest[n_side],
                                       rest[n_side + 1:2 * n_side + 1], rest[-1])
    j = pl.program_id(1)

    @pl.when(j == 0)
    def _():
        h = _norm_modulate(x_ref[...], g_ref[...],
                           mod_ref[3 * sub:3 * sub + 1, :],
                           mod_ref[3 * sub + 1:3 * sub + 2, :])
        h_ref[...] = h.astype(BF16)
        o_ref[...] = jnp.zeros_like(o_ref)

    h = h_ref[...]
    g = jnp.dot(h, wg_ref[...].astype(BF16), preferred_element_type=F32)
    u = jnp.dot(h, wu_ref[...].astype(BF16), preferred_element_type=F32)
    a = (g * jax.nn.sigmoid(g) * u).astype(BF16)
    o_ref[...] += jnp.dot(a, wd_ref[...].astype(BF16), preferred_element_type=F32)

    for src, dst in zip(side_in, side_out):
        dst[...] = src[...].astype(BF16)

    @pl.when(j == pl.num_programs(1) - 1)
    def _():
        gate = mod_ref[3 * sub + 2:3 * sub + 3, :]
        o_ref[...] = x_ref[...] + (0.5 * gate) * o_ref[...]


def _next_tile_chunk_index(ni, nchunk):
    def index(i, j):
        return jnp.minimum(i + 1, ni - 1) * nchunk + jnp.minimum(j, nchunk - 1), 0
    return index


def _ffn_pipe_kernel(xn_ref, x0_ref, mod_ref, g_ref, wg_ref, wu_ref, wd_ref, o_ref,
                     h_even, h_odd, xs_scr, *, sub, nchunk):
    i, j = pl.program_id(0), pl.program_id(1)
    shift = mod_ref[3 * sub:3 * sub + 1, :]
    scale = mod_ref[3 * sub + 1:3 * sub + 2, :]
    half_gate = 0.5 * mod_ref[3 * sub + 2:3 * sub + 3, :]

    @pl.when(j == 0)
    def _():
        @pl.when(i == 0)
        def _():
            x0 = x0_ref[...]
            h_even[...] = _norm_modulate(x0, g_ref[...], shift, scale).astype(BF16)
            o_ref[...] = x0

        @pl.when(i > 0)
        def _():
            o_ref[...] = xs_scr[...]

    def step(h_cur, h_next):
        rc = xn_ref.shape[0]
        r0 = pl.multiple_of(jnp.minimum(j, nchunk - 1) * rc, rc)
        xn = xn_ref[...]
        h_next[pl.ds(r0, rc), :] = _norm_modulate(xn, g_ref[...], shift, scale).astype(BF16)
        xs_scr[pl.ds(r0, rc), :] = xn

        h = h_cur[...]
        g = jnp.dot(h, wg_ref[...], preferred_element_type=F32)
        u = jnp.dot(h, wu_ref[...], preferred_element_type=F32)
        a = (g * jax.nn.sigmoid(g) * u).astype(BF16)
        o_ref[...] += jnp.dot(a, wd_ref[...], preferred_element_type=F32) * half_gate

    pl.when(i % 2 == 0)(lambda: step(h_even, h_odd))
    pl.when(i % 2 == 1)(lambda: step(h_odd, h_even))


def _ffn_pipe(x, mod, gain, w_gu, w_down, *, sub, tm, tf, nchunk):
    s, d = x.shape
    dff = w_down.shape[0]
    ni, nf = s // tm, dff // tf
    assert nchunk <= nf and tm % nchunk == 0
    return pl.pallas_call(
        functools.partial(_ffn_pipe_kernel, sub=sub, nchunk=nchunk),
        grid=(ni, nf),
        in_specs=[
            pl.BlockSpec((tm // nchunk, d), _next_tile_chunk_index(ni, nchunk)),
            pl.BlockSpec((tm, d), lambda i, j: (0, 0), pipeline_mode=pl.Buffered(1)),
            pl.BlockSpec(mod.shape, lambda i, j: (0, 0)),
            pl.BlockSpec((1, d), lambda i, j: (0, 0)),
            pl.BlockSpec((d, tf), lambda i, j: (0, j)),
            pl.BlockSpec((d, tf), lambda i, j: (0, j + nf)),
            pl.BlockSpec((tf, d), lambda i, j: (j, 0)),
        ],
        out_specs=pl.BlockSpec((tm, d), lambda i, j: (i, 0)),
        out_shape=jax.ShapeDtypeStruct((s, d), F32),
        scratch_shapes=[pltpu.VMEM((tm, d), BF16), pltpu.VMEM((tm, d), BF16),
                        pltpu.VMEM((tm, d), F32)],
        compiler_params=pltpu.CompilerParams(
            dimension_semantics=("arbitrary", "arbitrary"), vmem_limit_bytes=VMEM_LIMIT),
        name=f"ffn{sub}",
    )(x, x, mod, gain, w_gu, w_gu, w_down)


def _side_cast_blocking(rows, cols, n_steps):
    best = None
    for bc in range(LANES, cols + 1, LANES):
        for br in range(BF16_SUBLANES, rows + 1, BF16_SUBLANES):
            if cols % bc or rows % br or (rows // br) * (cols // bc) > n_steps:
                continue
            if best is None or br * bc < best[0] * best[1]:
                best = (br, bc)
    assert best is not None, (rows, cols, n_steps)
    return best


def _ffn(x, mod, gain, w_gu, w_down, side=(), *, layer, sub, tm, tf, x_buffers):
    s, d = x.shape
    dff = w_down.shape[1]
    ni, nf = s // tm, dff // tf

    side_specs_in, side_specs_out, side_shapes = [], [], []
    for w in side:
        rows, cols = w.shape[1:]
        br, bc = _side_cast_blocking(rows, cols, ni * nf)
        ncb, last = cols // bc, (rows // br) * (cols // bc) - 1

        def block_index(i, j, ncb=ncb, last=last):
            t = jnp.minimum(i * nf + j, last)
            return t // ncb, t % ncb

        side_specs_in.append(pl.BlockSpec(
            (None, br, bc), lambda i, j, f=block_index: (layer, *f(i, j))))
        side_specs_out.append(pl.BlockSpec((br, bc), block_index))
        side_shapes.append(jax.ShapeDtypeStruct((rows, cols), BF16))

    x_mode = {} if x_buffers == 2 else dict(pipeline_mode=pl.Buffered(x_buffers))
    outs = pl.pallas_call(
        functools.partial(_ffn_kernel, sub=sub, n_side=len(side)),
        grid=(ni, nf),
        in_specs=[
            pl.BlockSpec((tm, d), lambda i, j: (i, 0), **x_mode),
            pl.BlockSpec(mod.shape, lambda i, j: (0, 0)),
            pl.BlockSpec((1, d), lambda i, j: (0, 0)),
            pl.BlockSpec((None, d, tf), lambda i, j: (layer, 0, j)),
            pl.BlockSpec((None, d, tf), lambda i, j: (layer, 0, j + nf)),
            pl.BlockSpec((None, tf, d), lambda i, j: (layer, j, 0)),
            *side_specs_in,
        ],
        out_specs=[pl.BlockSpec((tm, d), lambda i, j: (i, 0)), *side_specs_out],
        out_shape=[jax.ShapeDtypeStruct((s, d), F32), *side_shapes],
        scratch_shapes=[pltpu.VMEM((tm, d), BF16)],
        compiler_params=pltpu.CompilerParams(
            dimension_semantics=("arbitrary", "arbitrary"), vmem_limit_bytes=VMEM_LIMIT),
        name=f"ffn{sub}",
    )(x, mod, gain, w_gu, w_gu, w_down, *side)
    return outs[0], outs[1:]


def _inproj_kernel(xn_ref, x0_ref, mod_ref, g_ref, w_ref, o_ref, h_even, h_odd, *,
                   sub, nchunk):
    i, j = pl.program_id(0), pl.program_id(1)
    shift = mod_ref[3 * sub:3 * sub + 1, :]
    scale = mod_ref[3 * sub + 1:3 * sub + 2, :]

    @pl.when(jnp.logical_and(i == 0, j == 0))
    def _():
        h_even[...] = _norm_modulate(x0_ref[...], g_ref[...], shift, scale).astype(BF16)

    def step(h_cur, h_next):
        rc = xn_ref.shape[0]
        r0 = pl.multiple_of(jnp.minimum(j, nchunk - 1) * rc, rc)
        cw = MXU_COLS
        pr = rc // (o_ref.shape[1] // cw - 1)
        for p in range(o_ref.shape[1] // cw):
            cols = slice(p * cw, (p + 1) * cw)
            o_ref[:, cols] = jnp.dot(h_cur[...], w_ref[:, cols], preferred_element_type=F32)
            if p * pr < rc:
                h_next[pl.ds(r0 + p * pr, pr), :] = _norm_modulate(
                    xn_ref[p * pr:(p + 1) * pr, :], g_ref[...], shift, scale).astype(BF16)

    pl.when(i % 2 == 0)(lambda: step(h_even, h_odd))
    pl.when(i % 2 == 1)(lambda: step(h_odd, h_even))


def _inproj(x, mod, gain, w, *, sub, tm, tn, nchunk):
    s, d = x.shape
    n = w.shape[1]
    ni, nj = s // tm, n // tn
    assert nchunk <= nj and tm % nchunk == 0
    return pl.pallas_call(
        functools.partial(_inproj_kernel, sub=sub, nchunk=nchunk),
        grid=(ni, nj),
        in_specs=[
            pl.BlockSpec((tm // nchunk, d), _next_tile_chunk_index(ni, nchunk)),
            pl.BlockSpec((tm, d), lambda i, j: (0, 0), pipeline_mode=pl.Buffered(1)),
            pl.BlockSpec(mod.shape, lambda i, j: (0, 0)),
            pl.BlockSpec((1, d), lambda i, j: (0, 0)),
            pl.BlockSpec((d, tn), lambda i, j: (0, j)),
        ],
        out_specs=pl.BlockSpec((tm, tn), lambda i, j: (i, j)),
        out_shape=jax.ShapeDtypeStruct((s, n), F32),
        scratch_shapes=[pltpu.VMEM((tm, d), BF16), pltpu.VMEM((tm, d), BF16)],
        compiler_params=pltpu.CompilerParams(
            dimension_semantics=("arbitrary", "arbitrary"), vmem_limit_bytes=VMEM_LIMIT),
        name="in_proj",
    )(x, x, mod, gain, w)


def _rms_over_rows(t):
    ms = jnp.mean(t * t, axis=1, keepdims=True)
    return t * lax.rsqrt(ms + EPS)


def _attn_kernel(q_ref, kv_ref, kvp_ref, qg_ref, kg_ref, bucket_ref, rel_ref, sink_ref,
                 o_ref, bias_scr, sink_scr, *, nblk):
    i = pl.program_id(0)

    from_prev = (lax.broadcasted_iota(jnp.int32, (BLK, BLK), 0)
                 > lax.broadcasted_iota(jnp.int32, (BLK, BLK), 1))

    @pl.when(i == 0)
    def _():
        bucket = bucket_ref[...]
        for h in range(N_HEADS):
            kh, g = divmod(h, GQA_GROUP)
            lanes = slice(g * BLK, (g + 1) * BLK)
            acc = jnp.zeros((BLK, BLK), F32)
            for b in range(NUM_BUCKETS):
                acc = jnp.where(bucket == b, rel_ref[b, h] * LOG2E, acc)
            bias_scr[kh, :, lanes] = acc
            bias_scr[N_KV_HEADS + kh, :, lanes] = jnp.where(from_prev, NEG_INF, acc)
            sink_scr[kh, :, lanes] = jnp.full((1, BLK), sink_ref[h] * LOG2E, F32)

    t = (nblk + 1) * BLK
    kv_t = jnp.concatenate([kvp_ref[...], kv_ref[...]], axis=0).T
    k_t = kv_t[0:N_KV_HEADS * HEAD_DIM].reshape(N_KV_HEADS, HEAD_DIM, t)
    k_gain = kg_ref[...] * qg_ref[...] * (HEAD_DIM ** -0.5 * LOG2E)
    kn = (_rms_over_rows(k_t) * k_gain).reshape(N_KV_HEADS * HEAD_DIM, t).T
    lane = lax.broadcasted_iota(jnp.int32, kn.shape, 1)
    k_sel = [jnp.where((lane // HEAD_DIM) == kh, kn, 0.0).astype(BF16)
             for kh in range(N_KV_HEADS)]
    v_t = [kv_t[(N_KV_HEADS + kh) * HEAD_DIM:(N_KV_HEADS + kh + 1) * HEAD_DIM].astype(BF16)
           for kh in range(N_KV_HEADS)]

    for b in range(nblk):
        rows = slice(b * BLK, (b + 1) * BLK)
        band = slice(b * BLK, (b + 2) * BLK)
        q_t = q_ref[rows, :].T.reshape(N_HEADS, HEAD_DIM, BLK)
        qn = _rms_over_rows(q_t).astype(BF16)
        out_t = []
        for kh in range(N_KV_HEADS):
            q_grp = jnp.concatenate([qn[kh * GQA_GROUP + g] for g in range(GQA_GROUP)],
                                    axis=1)
            q_dup = jnp.concatenate([q_grp] * N_KV_HEADS, axis=0)
            s = jnp.dot(k_sel[kh][band], q_dup, preferred_element_type=F32)
            table = kh + N_KV_HEADS * (i == 0).astype(jnp.int32) if b == 0 else kh
            e_parts, denom_parts = [], []
            for g in range(GQA_GROUP):
                lanes = slice(g * BLK, (g + 1) * BLK)
                sh = jnp.where(from_prev, s[0:BLK, lanes], s[BLK:2 * BLK, lanes])
                sh = sh + bias_scr[table, :, lanes]
                sink = sink_scr[kh, :, lanes]
                m = jnp.maximum(jnp.max(sh, axis=0, keepdims=True), sink)
                e = jnp.exp2(sh - m)
                denom_parts.append(jnp.sum(e, axis=0, keepdims=True) + jnp.exp2(sink - m))
                e_parts.append(jnp.concatenate(
                    [jnp.where(from_prev, e, 0.0), jnp.where(from_prev, 0.0, e)],
                    axis=0).astype(BF16))
            e = jnp.concatenate(e_parts, axis=1)
            denom = jnp.concatenate(denom_parts, axis=1)
            o = jnp.dot(v_t[kh][:, band], e, preferred_element_type=F32)
            o = o * (1.0 / denom)
            out_t += [o[:, g * BLK:(g + 1) * BLK] for g in range(GQA_GROUP)]
        o_ref[rows, :] = jnp.concatenate(out_t, axis=0).T.astype(BF16)


def _attention(z, q_gain, k_gain, rel_bias, sinks, *, tq):
    s = z.shape[0]
    nblk = tq // BLK
    gw = GQA_GROUP * BLK
    q_col = POOL_W // ATTN_W
    kv_col = (POOL_W + ATTN_W) // KV_W
    bucket_t = _rel_bucket_band().T
    jm, ql = np.meshgrid(np.arange(BLK), np.arange(BLK), indexing="ij")
    bucket_fold = jnp.asarray(np.where(jm > ql, bucket_t[:BLK], bucket_t[BLK:]))
    return pl.pallas_call(
        functools.partial(_attn_kernel, nblk=nblk),
        grid=(s // tq,),
        in_specs=[
            pl.BlockSpec((tq, ATTN_W), lambda i: (i, q_col)),
            pl.BlockSpec((tq, KV_W), lambda i: (i, kv_col)),
            pl.BlockSpec((BLK, KV_W), lambda i: (jnp.maximum(i * nblk - 1, 0), kv_col)),
            pl.BlockSpec((HEAD_DIM, 1), lambda i: (0, 0)),
            pl.BlockSpec((HEAD_DIM, 1), lambda i: (0, 0)),
            pl.BlockSpec((BLK, BLK), lambda i: (0, 0)),
            pl.BlockSpec(memory_space=pltpu.SMEM),
            pl.BlockSpec(memory_space=pltpu.SMEM),
        ],
        out_specs=pl.BlockSpec((tq, ATTN_W), lambda i: (i, 0)),
        out_shape=jax.ShapeDtypeStruct((s, ATTN_W), BF16),
        scratch_shapes=[
            pltpu.VMEM((2 * N_KV_HEADS, BLK, gw), F32),
            pltpu.VMEM((N_KV_HEADS, 1, gw), F32),
        ],
        compiler_params=pltpu.CompilerParams(
            dimension_semantics=("arbitrary",), vmem_limit_bytes=VMEM_LIMIT),
        name="swa_attention",
    )(z, z, z, q_gain.reshape(HEAD_DIM, 1), k_gain.reshape(HEAD_DIM, 1), bucket_fold,
      rel_bias, sinks)


def _mix_kernel(u_ref, up_ref, ga_ref, gb_ref, attn_ref, x_ref, mod_ref, pmix_ref,
                pscale_ref, wpool_ref, wattn_ref, wo_ref, o_ref, ubuf, *, sub, tm):
    i = pl.program_id(0)
    d = o_ref.shape[1]
    halo = up_ref[...]
    ubuf[0:POOL_HALO, :] = jnp.where(i == 0, 0.0, halo)
    ubuf[POOL_HALO:, :] = u_ref[...]

    t1 = i * tm + lax.broadcasted_iota(jnp.int32, (tm, POOL_GROUP_W), 0) + 1
    mixed, merged_attn = [], []
    dc = d // len(POOL_WINDOWS)
    for gi, w in enumerate(POOL_WINDOWS):
        out_cols = slice(gi * dc, (gi + 1) * dc)
        y_attn = jnp.dot(attn_ref[...], wattn_ref[:, out_cols], preferred_element_type=F32)
        merged_attn.append(jax.nn.sigmoid(gb_ref[:, out_cols]) * y_attn)

        cols = slice(gi * POOL_GROUP_W, (gi + 1) * POOL_GROUP_W)
        cur = ubuf[POOL_HALO:POOL_HALO + tm, cols]
        win = cur
        for dlt in range(1, w):
            win = win + ubuf[POOL_HALO - dlt:POOL_HALO - dlt + tm, cols]
        cnt = jnp.minimum(t1, w).astype(F32)
        pooled = (win / cnt - cur).astype(BF16)
        mg = jnp.dot(pooled, pmix_ref[gi], preferred_element_type=F32)
        mixed.append((mg * pscale_ref[:, cols]).astype(BF16))
    mixed = jnp.concatenate(mixed, axis=1)

    y_pool = jnp.dot(mixed, wpool_ref[...], preferred_element_type=F32)
    merged = jax.nn.sigmoid(ga_ref[...]) * y_pool + jnp.concatenate(merged_attn, axis=1)
    y = jnp.dot(merged.astype(BF16), wo_ref[...], preferred_element_type=F32)
    gate = mod_ref[3 * sub + 2:3 * sub + 3, :]
    o_ref[...] = x_ref[...] + gate * y


def _mix(z, attn, x, mod, pool_mix, pool_scale, w_pool_up, w_attn_up, w_o, *, sub, tm):
    s, d = x.shape
    const = dict(pipeline_mode=pl.Buffered(1))
    ga_off = POOL_W + ATTN_W + KV_W
    return pl.pallas_call(
        functools.partial(_mix_kernel, sub=sub, tm=tm),
        grid=(s // tm,),
        in_specs=[
            pl.BlockSpec((tm, POOL_W), lambda i: (i, 0)),
            pl.BlockSpec((POOL_HALO, POOL_W),
                         lambda i: (jnp.maximum(i * (tm // POOL_HALO) - 1, 0), 0)),
            pl.BlockSpec((pl.Element(tm), pl.Element(d)), lambda i: (i * tm, ga_off)),
            pl.BlockSpec((pl.Element(tm), pl.Element(d)), lambda i: (i * tm, ga_off + d)),
            pl.BlockSpec((tm, ATTN_W), lambda i: (i, 0)),
            pl.BlockSpec((tm, d), lambda i: (i, 0)),
            pl.BlockSpec(mod.shape, lambda i: (0, 0), **const),
            pl.BlockSpec(pool_mix.shape, lambda i: (0, 0, 0), **const),
            pl.BlockSpec((1, POOL_W), lambda i: (0, 0), **const),
            pl.BlockSpec(w_pool_up.shape, lambda i: (0, 0), **const),
            pl.BlockSpec(w_attn_up.shape, lambda i: (0, 0), **const),
            pl.BlockSpec(w_o.shape, lambda i: (0, 0), **const),
        ],
        out_specs=pl.BlockSpec((tm, d), lambda i: (i, 0)),
        out_shape=jax.ShapeDtypeStruct((s, d), F32),
        scratch_shapes=[pltpu.VMEM((tm + POOL_HALO, POOL_W), F32)],
        compiler_params=pltpu.CompilerParams(
            dimension_semantics=("arbitrary",), vmem_limit_bytes=VMEM_LIMIT),
        name="pool_merge_out",
    )(z, z, z, z, attn, x, mod, pool_mix, pool_scale, w_pool_up, w_attn_up, w_o)


def kernel(x, c, w_ada, b_ada, g_ffn1, w_ffn1_gu, w_ffn1_down, g_mix, w_in, pool_mix,
           pool_scale, w_pool_up, q_gain, k_gain, sinks, rel_bias, w_attn_up, w_o,
           g_ffn2, w_ffn2_gu, w_ffn2_down):
    b, s, d = x.shape
    depth = w_ada.shape[0]
    assert b == 1, "adaLN matvec and row tiling assume a single sequence"
    xs = x[0]
    for l in range(depth):
        mod = _ada(c.reshape(d, 1), w_ada[l:l + 1], b_ada[l:l + 1], tn=1024)
        mod = mod.reshape(3 * N_SUB, d)

        later = (w_ffn2_gu, w_ffn2_down, w_in, pool_mix.reshape(depth, POOL_W, POOL_GROUP_W),
                 w_pool_up, w_attn_up, w_o)
        xs, (gu2, down2, w_in_b, pmix_b, wpool_b, wattn_b, wo_b) = _ffn(
            xs, mod, g_ffn1[l:l + 1], w_ffn1_gu, w_ffn1_down, later,
            layer=l, sub=0, tm=1024, tf=256, x_buffers=1)

        z = _inproj(xs, mod, g_mix[l:l + 1], w_in_b, sub=1, tm=1024, tn=1280, nchunk=4)
        attn = _attention(z, q_gain[l], k_gain[l], rel_bias, sinks[l], tq=512)
        xs = _mix(z, attn, xs, mod, pmix_b.reshape(pool_mix.shape[1:]), pool_scale[l:l + 1],
                  wpool_b, wattn_b, wo_b, sub=1, tm=256)

        xs = _ffn_pipe(xs, mod, g_ffn2[l:l + 1], gu2, down2, sub=2, tm=512, tf=512, nchunk=8)
    return xs[None]
```

```python
import functools

import jax
import jax.numpy as jnp
import numpy as np
from jax import lax
from jax.experimental import pallas as pl
from jax.experimental.pallas import tpu as pltpu

F32 = jnp.float32
BF16 = jnp.bfloat16

N_SUB = 3
POOL_WINDOWS = (2, 4, 8, 16)
POOL_GROUP_W = 256
POOL_W = len(POOL_WINDOWS) * POOL_GROUP_W
HEAD_DIM = 64
N_HEADS = 16
N_KV_HEADS = 2
GQA_GROUP = N_HEADS // N_KV_HEADS
ATTN_W = N_HEADS * HEAD_DIM
KV_W = 2 * N_KV_HEADS * HEAD_DIM
BLK = 128
NUM_BUCKETS = 32
MAX_EXACT = NUM_BUCKETS // 2
REL_MAX_DIST = 128
EPS = 1e-6
NEG_INF = -1e30
LOG2E = float(np.log2(np.e))
POOL_HALO = 16

LANES = 128
BF16_SUBLANES = 16
MXU_COLS = 256

V7X_VMEM_BYTES = 64 * 1024 * 1024
VMEM_LIMIT = V7X_VMEM_BYTES - 8 * 1024 * 1024


def _rel_bucket_band():
    ql = np.arange(BLK)[:, None]
    j = np.arange(2 * BLK)[None, :]
    n = np.clip(BLK + ql - j, 0, None)
    nf = np.maximum(n, 1).astype(np.float32)
    large = MAX_EXACT + (np.log(nf / MAX_EXACT) / np.log(REL_MAX_DIST / MAX_EXACT)
                         * (NUM_BUCKETS - MAX_EXACT)).astype(np.int32)
    large = np.minimum(large, NUM_BUCKETS - 1)
    return np.where(n < MAX_EXACT, n, large).astype(np.int32)


def _norm_modulate(x, gain, shift, scale):
    ms = jnp.mean(x * x, axis=-1, keepdims=True)
    y = x * lax.rsqrt(ms + EPS) * gain
    return y * (1.0 + scale) + shift


def _ada_kernel(c_ref, w_ref, b_ref, o_ref):
    cc = c_ref[...]
    sc = cc * jax.nn.sigmoid(cc)
    o_ref[...] = jnp.sum(w_ref[...] * sc, axis=0, keepdims=True) + b_ref[...]


def _ada(c_col, w_ada, b_ada, *, tn):
    d, n = w_ada.shape[1], w_ada.shape[2]
    return pl.pallas_call(
        _ada_kernel,
        grid=(n // tn,),
        in_specs=[
            pl.BlockSpec((d, 1), lambda j: (0, 0)),
            pl.BlockSpec((None, d, tn), lambda j: (0, 0, j)),
            pl.BlockSpec((1, tn), lambda j: (0, j)),
        ],
        out_specs=pl.BlockSpec((1, tn), lambda j: (0, j)),
        out_shape=jax.ShapeDtypeStruct((1, n), F32),
        compiler_params=pltpu.CompilerParams(
            dimension_semantics=("arbitrary",), vmem_limit_bytes=VMEM_LIMIT),
        name="ada_ln",
    )(c_col, w_ada, b_ada)


def _ffn_kernel(x_ref, mod_ref, g_ref, wg_ref, wu_ref, wd_ref, *rest, sub, n_side,
                row_parts):
    side_in, o_ref, side_out, h_ref = (rest[:n_side], rest[n_side],
                                       rest[n_side + 1:2 * n_side + 1], rest[-1])
    j = pl.program_id(1)

    @pl.when(j == 0)
    def _():
        h = _norm_modulate(x_ref[...], g_ref[...],
                           mod_ref[3 * sub:3 * sub + 1, :],
                           mod_ref[3 * sub + 1:3 * sub + 2, :])
        h_ref[...] = h.astype(BF16)
        o_ref[...] = jnp.zeros_like(o_ref)

    wg, wu, wd = (w[...].astype(BF16) for w in (wg_ref, wu_ref, wd_ref))
    pm = o_ref.shape[0] // row_parts
    for r in range(row_parts):
        rows = slice(r * pm, (r + 1) * pm)
        h = h_ref[rows, :]
        g = jnp.dot(h, wg, preferred_element_type=F32)
        u = jnp.dot(h, wu, preferred_element_type=F32)
        a = (g * jax.nn.sigmoid(g) * u).astype(BF16)
        o_ref[rows, :] += jnp.dot(a, wd, preferred_element_type=F32)

    for src, dst in zip(side_in, side_out):
        dst[...] = src[...].astype(BF16)

    @pl.when(j == pl.num_programs(1) - 1)
    def _():
        gate = mod_ref[3 * sub + 2:3 * sub + 3, :]
        o_ref[...] = x_ref[...] + (0.5 * gate) * o_ref[...]


def _next_tile_chunk_index(ni, nchunk):
    def index(i, j):
        return jnp.minimum(i + 1, ni - 1) * nchunk + jnp.minimum(j, nchunk - 1), 0
    return index


def _ffn_pipe_kernel(xn_ref, x_hbm, mod_ref, g_ref, wg_ref, wu_ref, wd_ref, o_ref,
                     h_even, h_odd, xs_scr, *, sub, nchunk, row_parts):
    i, j = pl.program_id(0), pl.program_id(1)
    tm = o_ref.shape[0]
    shift = mod_ref[3 * sub:3 * sub + 1, :]
    scale = mod_ref[3 * sub + 1:3 * sub + 2, :]
    half_gate = 0.5 * mod_ref[3 * sub + 2:3 * sub + 3, :]

    @pl.when(j == 0)
    def _():
        @pl.when(i == 0)
        def _():
            pltpu.sync_copy(x_hbm.at[pl.ds(0, tm), :], xs_scr)
            h_even[...] = _norm_modulate(xs_scr[...], g_ref[...], shift, scale).astype(BF16)

        o_ref[...] = xs_scr[...]

    def step(h_cur, h_next):
        rc = xn_ref.shape[0]
        r0 = pl.multiple_of(jnp.minimum(j, nchunk - 1) * rc, rc)
        xn = xn_ref[...]
        h_next[pl.ds(r0, rc), :] = _norm_modulate(xn, g_ref[...], shift, scale).astype(BF16)
        xs_scr[pl.ds(r0, rc), :] = xn

        pm = tm // row_parts
        for r in range(row_parts):
            rows = slice(r * pm, (r + 1) * pm)
            h = h_cur[rows, :]
            g = jnp.dot(h, wg_ref[...], preferred_element_type=F32)
            u = jnp.dot(h, wu_ref[...], preferred_element_type=F32)
            a = (g * jax.nn.sigmoid(g) * u).astype(BF16)
            o_ref[rows, :] += jnp.dot(a, wd_ref[...], preferred_element_type=F32) * half_gate

    pl.when(i % 2 == 0)(lambda: step(h_even, h_odd))
    pl.when(i % 2 == 1)(lambda: step(h_odd, h_even))


def _ffn_pipe(x, mod, gain, w_gu, w_down, *, sub, tm, tf, nchunk, row_parts):
    s, d = x.shape
    dff = w_down.shape[0]
    ni, nf = s // tm, dff // tf
    assert nchunk <= nf and tm % nchunk == 0 and tm % row_parts == 0
    return pl.pallas_call(
        functools.partial(_ffn_pipe_kernel, sub=sub, nchunk=nchunk, row_parts=row_parts),
        grid=(ni, nf),
        in_specs=[
            pl.BlockSpec((tm // nchunk, d), _next_tile_chunk_index(ni, nchunk)),
            pl.BlockSpec(memory_space=pl.ANY),
            pl.BlockSpec(mod.shape, lambda i, j: (0, 0)),
            pl.BlockSpec((1, d), lambda i, j: (0, 0)),
            pl.BlockSpec((d, tf), lambda i, j: (0, j)),
            pl.BlockSpec((d, tf), lambda i, j: (0, j + nf)),
            pl.BlockSpec((tf, d), lambda i, j: (j, 0)),
        ],
        out_specs=pl.BlockSpec((tm, d), lambda i, j: (i, 0)),
        out_shape=jax.ShapeDtypeStruct((s, d), F32),
        scratch_shapes=[pltpu.VMEM((tm, d), BF16), pltpu.VMEM((tm, d), BF16),
                        pltpu.VMEM((tm, d), F32)],
        compiler_params=pltpu.CompilerParams(
            dimension_semantics=("arbitrary", "arbitrary"), vmem_limit_bytes=VMEM_LIMIT),
        name=f"ffn{sub}",
    )(x, x, mod, gain, w_gu, w_gu, w_down)


def _side_cast_blocking(rows, cols, n_steps):
    best = None
    for bc in range(LANES, cols + 1, LANES):
        for br in range(BF16_SUBLANES, rows + 1, BF16_SUBLANES):
            if cols % bc or rows % br or (rows // br) * (cols // bc) > n_steps:
                continue
            if best is None or br * bc < best[0] * best[1]:
                best = (br, bc)
    assert best is not None, (rows, cols, n_steps)
    return best


def _ffn(x, mod, gain, w_gu, w_down, side=(), *, layer, sub, tm, tf, x_buffers, row_parts):
    s, d = x.shape
    dff = w_down.shape[1]
    ni, nf = s // tm, dff // tf

    side_specs_in, side_specs_out, side_shapes = [], [], []
    for w in side:
        rows, cols = w.shape[1:]
        br, bc = _side_cast_blocking(rows, cols, ni * nf)
        ncb, last = cols // bc, (rows // br) * (cols // bc) - 1

        def block_index(i, j, ncb=ncb, last=last):
            t = jnp.minimum(i * nf + j, last)
            return t // ncb, t % ncb

        side_specs_in.append(pl.BlockSpec(
            (None, br, bc), lambda i, j, f=block_index: (layer, *f(i, j))))
        side_specs_out.append(pl.BlockSpec((br, bc), block_index))
        side_shapes.append(jax.ShapeDtypeStruct((rows, cols), BF16))

    x_mode = {} if x_buffers == 2 else dict(pipeline_mode=pl.Buffered(x_buffers))
    outs = pl.pallas_call(
        functools.partial(_ffn_kernel, sub=sub, n_side=len(side), row_parts=row_parts),
        grid=(ni, nf),
        in_specs=[
            pl.BlockSpec((tm, d), lambda i, j: (i, 0), **x_mode),
            pl.BlockSpec(mod.shape, lambda i, j: (0, 0)),
            pl.BlockSpec((1, d), lambda i, j: (0, 0)),
            pl.BlockSpec((None, d, tf), lambda i, j: (layer, 0, j)),
            pl.BlockSpec((None, d, tf), lambda i, j: (layer, 0, j + nf)),
            pl.BlockSpec((None, tf, d), lambda i, j: (layer, j, 0)),
            *side_specs_in,
        ],
        out_specs=[pl.BlockSpec((tm, d), lambda i, j: (i, 0)), *side_specs_out],
        out_shape=[jax.ShapeDtypeStruct((s, d), F32), *side_shapes],
        scratch_shapes=[pltpu.VMEM((tm, d), BF16)],
        compiler_params=pltpu.CompilerParams(
            dimension_semantics=("arbitrary", "arbitrary"), vmem_limit_bytes=VMEM_LIMIT),
        name=f"ffn{sub}",
    )(x, mod, gain, w_gu, w_gu, w_down, *side)
    return outs[0], outs[1:]


def _inproj_kernel(xn_ref, x0_ref, mod_ref, g_ref, w_ref, o_ref, h_even, h_odd, *,
                   sub, nchunk):
    i, j = pl.program_id(0), pl.program_id(1)
    shift = mod_ref[3 * sub:3 * sub + 1, :]
    scale = mod_ref[3 * sub + 1:3 * sub + 2, :]

    @pl.when(jnp.logical_and(i == 0, j == 0))
    def _():
        h_even[...] = _norm_modulate(x0_ref[...], g_ref[...], shift, scale).astype(BF16)

    def step(h_cur, h_next):
        rc = xn_ref.shape[0]
        r0 = pl.multiple_of(jnp.minimum(j, nchunk - 1) * rc, rc)
        cw = MXU_COLS
        pr = rc // (o_ref.shape[1] // cw - 1)
        for p in range(o_ref.shape[1] // cw):
            cols = slice(p * cw, (p + 1) * cw)
            o_ref[:, cols] = jnp.dot(h_cur[...], w_ref[:, cols], preferred_element_type=F32)
            if p * pr < rc:
                h_next[pl.ds(r0 + p * pr, pr), :] = _norm_modulate(
                    xn_ref[p * pr:(p + 1) * pr, :], g_ref[...], shift, scale).astype(BF16)

    pl.when(i % 2 == 0)(lambda: step(h_even, h_odd))
    pl.when(i % 2 == 1)(lambda: step(h_odd, h_even))


def _inproj(x, mod, gain, w, *, sub, tm, tn, nchunk):
    s, d = x.shape
    n = w.shape[1]
    ni, nj = s // tm, n // tn
    assert nchunk <= nj and tm % nchunk == 0
    return pl.pallas_call(
        functools.partial(_inproj_kernel, sub=sub, nchunk=nchunk),
        grid=(ni, nj),
        in_specs=[
            pl.BlockSpec((tm // nchunk, d), _next_tile_chunk_index(ni, nchunk)),
            pl.BlockSpec((tm, d), lambda i, j: (0, 0), pipeline_mode=pl.Buffered(1)),
            pl.BlockSpec(mod.shape, lambda i, j: (0, 0)),
            pl.BlockSpec((1, d), lambda i, j: (0, 0)),
            pl.BlockSpec((d, tn), lambda i, j: (0, j)),
        ],
        out_specs=pl.BlockSpec((tm, tn), lambda i, j: (i, j)),
        out_shape=jax.ShapeDtypeStruct((s, n), F32),
        scratch_shapes=[pltpu.VMEM((tm, d), BF16), pltpu.VMEM((tm, d), BF16)],
        compiler_params=pltpu.CompilerParams(
            dimension_semantics=("arbitrary", "arbitrary"), vmem_limit_bytes=VMEM_LIMIT),
        name="in_proj",
    )(x, x, mod, gain, w)


def _rms_over_rows(t):
    ms = jnp.mean(t * t, axis=1, keepdims=True)
    return t * lax.rsqrt(ms + EPS)


def _attn_kernel(q_ref, kv_ref, kvp_ref, qg_ref, kg_ref, bucket_ref, rel_ref, sink_ref,
                 o_ref, bias_scr, sink_scr, *, nblk):
    i = pl.program_id(0)

    from_prev = (lax.broadcasted_iota(jnp.int32, (BLK, BLK), 0)
                 > lax.broadcasted_iota(jnp.int32, (BLK, BLK), 1))

    @pl.when(i == 0)
    def _():
        bucket = bucket_ref[...]
        for h in range(N_HEADS):
            kh, g = divmod(h, GQA_GROUP)
            lanes = slice(g * BLK, (g + 1) * BLK)
            acc = jnp.zeros((BLK, BLK), F32)
            for b in range(NUM_BUCKETS):
                acc = jnp.where(bucket == b, rel_ref[b, h] * LOG2E, acc)
            bias_scr[kh, :, lanes] = acc
            bias_scr[N_KV_HEADS + kh, :, lanes] = jnp.where(from_prev, NEG_INF, acc)
            sink_scr[kh, :, lanes] = jnp.full((1, BLK), sink_ref[h] * LOG2E, F32)

    t = (nblk + 1) * BLK
    kv_t = jnp.concatenate([kvp_ref[...], kv_ref[...]], axis=0).T
    k_t = kv_t[0:N_KV_HEADS * HEAD_DIM].reshape(N_KV_HEADS, HEAD_DIM, t)
    k_gain = kg_ref[...] * qg_ref[...] * (HEAD_DIM ** -0.5 * LOG2E)
    kn = (_rms_over_rows(k_t) * k_gain).reshape(N_KV_HEADS * HEAD_DIM, t).T
    lane = lax.broadcasted_iota(jnp.int32, kn.shape, 1)
    k_sel = [jnp.where((lane // HEAD_DIM) == kh, kn, 0.0).astype(BF16)
             for kh in range(N_KV_HEADS)]
    v_t = [kv_t[(N_KV_HEADS + kh) * HEAD_DIM:(N_KV_HEADS + kh + 1) * HEAD_DIM].astype(BF16)
           for kh in range(N_KV_HEADS)]

    for b in range(nblk):
        rows = slice(b * BLK, (b + 1) * BLK)
        band = slice(b * BLK, (b + 2) * BLK)
        q_t = q_ref[rows, :].T.reshape(N_HEADS, HEAD_DIM, BLK)
        qn = _rms_over_rows(q_t).astype(BF16)
        out_t = []
        for kh in range(N_KV_HEADS):
            q_grp = jnp.concatenate([qn[kh * GQA_GROUP + g] for g in range(GQA_GROUP)],
                                    axis=1)
            q_dup = jnp.concatenate([q_grp] * N_KV_HEADS, axis=0)
            s = jnp.dot(k_sel[kh][band], q_dup, preferred_element_type=F32)
            table = kh + N_KV_HEADS * (i == 0).astype(jnp.int32) if b == 0 else kh
            e_parts, denom_parts = [], []
            for g in range(GQA_GROUP):
                lanes = slice(g * BLK, (g + 1) * BLK)
                sh = jnp.where(from_prev, s[0:BLK, lanes], s[BLK:2 * BLK, lanes])
                sh = sh + bias_scr[table, :, lanes]
                sink = sink_scr[kh, :, lanes]
                m = jnp.maximum(jnp.max(sh, axis=0, keepdims=True), sink)
                e = jnp.exp2(sh - m)
                denom_parts.append(jnp.sum(e, axis=0, keepdims=True) + jnp.exp2(sink - m))
                e_parts.append(jnp.concatenate(
                    [jnp.where(from_prev, e, 0.0), jnp.where(from_prev, 0.0, e)],
                    axis=0).astype(BF16))
            e = jnp.concatenate(e_parts, axis=1)
            denom = jnp.concatenate(denom_parts, axis=1)
            o = jnp.dot(v_t[kh][:, band], e, preferred_element_type=F32)
            o = o * (1.0 / denom)
            out_t += [o[:, g * BLK:(g + 1) * BLK] for g in range(GQA_GROUP)]
        o_ref[rows, :] = jnp.concatenate(out_t, axis=0).T.astype(BF16)


def _attention(z, q_gain, k_gain, rel_bias, sinks, *, tq):
    s = z.shape[0]
    nblk = tq // BLK
    gw = GQA_GROUP * BLK
    q_col = POOL_W // ATTN_W
    kv_col = (POOL_W + ATTN_W) // KV_W
    bucket_t = _rel_bucket_band().T
    jm, ql = np.meshgrid(np.arange(BLK), np.arange(BLK), indexing="ij")
    bucket_fold = jnp.asarray(np.where(jm > ql, bucket_t[:BLK], bucket_t[BLK:]))
    return pl.pallas_call(
        functools.partial(_attn_kernel, nblk=nblk),
        grid=(s // tq,),
        in_specs=[
            pl.BlockSpec((tq, ATTN_W), lambda i: (i, q_col)),
            pl.BlockSpec((tq, KV_W), lambda i: (i, kv_col)),
            pl.BlockSpec((BLK, KV_W), lambda i: (jnp.maximum(i * nblk - 1, 0), kv_col)),
            pl.BlockSpec((HEAD_DIM, 1), lambda i: (0, 0)),
            pl.BlockSpec((HEAD_DIM, 1), lambda i: (0, 0)),
            pl.BlockSpec((BLK, BLK), lambda i: (0, 0)),
            pl.BlockSpec(memory_space=pltpu.SMEM),
            pl.BlockSpec(memory_space=pltpu.SMEM),
        ],
        out_specs=pl.BlockSpec((tq, ATTN_W), lambda i: (i, 0)),
        out_shape=jax.ShapeDtypeStruct((s, ATTN_W), BF16),
        scratch_shapes=[
            pltpu.VMEM((2 * N_KV_HEADS, BLK, gw), F32),
            pltpu.VMEM((N_KV_HEADS, 1, gw), F32),
        ],
        compiler_params=pltpu.CompilerParams(
            dimension_semantics=("arbitrary",), vmem_limit_bytes=VMEM_LIMIT),
        name="swa_attention",
    )(z, z, z, q_gain.reshape(HEAD_DIM, 1), k_gain.reshape(HEAD_DIM, 1), bucket_fold,
      rel_bias, sinks)


def _mix_kernel(u_ref, up_ref, ga_ref, gb_ref, attn_ref, x_ref, mod_ref, pmix_ref,
                pscale_ref, wpool_ref, wattn_ref, wo_ref, o_ref, ubuf, *, sub, tm):
    i = pl.program_id(0)
    d = o_ref.shape[1]
    halo = up_ref[...]
    ubuf[0:POOL_HALO, :] = jnp.where(i == 0, 0.0, halo)
    ubuf[POOL_HALO:, :] = u_ref[...]

    t1 = i * tm + lax.broadcasted_iota(jnp.int32, (tm, POOL_GROUP_W), 0) + 1
    mixed, merged_attn = [], []
    dc = d // len(POOL_WINDOWS)
    for gi, w in enumerate(POOL_WINDOWS):
        out_cols = slice(gi * dc, (gi + 1) * dc)
        y_attn = jnp.dot(attn_ref[...], wattn_ref[:, out_cols], preferred_element_type=F32)
        merged_attn.append(jax.nn.sigmoid(gb_ref[:, out_cols]) * y_attn)

        cols = slice(gi * POOL_GROUP_W, (gi + 1) * POOL_GROUP_W)
        cur = ubuf[POOL_HALO:POOL_HALO + tm, cols]
        win = cur
        for dlt in range(1, w):
            win = win + ubuf[POOL_HALO - dlt:POOL_HALO - dlt + tm, cols]
        cnt = jnp.minimum(t1, w).astype(F32)
        pooled = (win / cnt - cur).astype(BF16)
        mg = jnp.dot(pooled, pmix_ref[gi], preferred_element_type=F32)
        mixed.append((mg * pscale_ref[:, cols]).astype(BF16))
    mixed = jnp.concatenate(mixed, axis=1)

    y_pool = jnp.dot(mixed, wpool_ref[...], preferred_element_type=F32)
    merged = jax.nn.sigmoid(ga_ref[...]) * y_pool + jnp.concatenate(merged_attn, axis=1)
    y = jnp.dot(merged.astype(BF16), wo_ref[...], preferred_element_type=F32)
    gate = mod_ref[3 * sub + 2:3 * sub + 3, :]
    o_ref[...] = x_ref[...] + gate * y


def _mix(z, attn, x, mod, pool_mix, pool_scale, w_pool_up, w_attn_up, w_o, *, sub, tm):
    s, d = x.shape
    const = dict(pipeline_mode=pl.Buffered(1))
    ga_off = POOL_W + ATTN_W + KV_W
    return pl.pallas_call(
        functools.partial(_mix_kernel, sub=sub, tm=tm),
        grid=(s // tm,),
        in_specs=[
            pl.BlockSpec((tm, POOL_W), lambda i: (i, 0)),
            pl.BlockSpec((POOL_HALO, POOL_W),
                         lambda i: (jnp.maximum(i * (tm // POOL_HALO) - 1, 0), 0)),
            pl.BlockSpec((pl.Element(tm), pl.Element(d)), lambda i: (i * tm, ga_off)),
            pl.BlockSpec((pl.Element(tm), pl.Element(d)), lambda i: (i * tm, ga_off + d)),
            pl.BlockSpec((tm, ATTN_W), lambda i: (i, 0)),
            pl.BlockSpec((tm, d), lambda i: (i, 0)),
            pl.BlockSpec(mod.shape, lambda i: (0, 0), **const),
            pl.BlockSpec(pool_mix.shape, lambda i: (0, 0, 0), **const),
            pl.BlockSpec((1, POOL_W), lambda i: (0, 0), **const),
            pl.BlockSpec(w_pool_up.shape, lambda i: (0, 0), **const),
            pl.BlockSpec(w_attn_up.shape, lambda i: (0, 0), **const),
            pl.BlockSpec(w_o.shape, lambda i: (0, 0), **const),
        ],
        out_specs=pl.BlockSpec((tm, d), lambda i: (i, 0)),
        out_shape=jax.ShapeDtypeStruct((s, d), F32),
        scratch_shapes=[pltpu.VMEM((tm + POOL_HALO, POOL_W), F32)],
        compiler_params=pltpu.CompilerParams(
            dimension_semantics=("arbitrary",), vmem_limit_bytes=VMEM_LIMIT),
        name="pool_merge_out",
    )(z, z, z, z, attn, x, mod, pool_mix, pool_scale, w_pool_up, w_attn_up, w_o)


def kernel(x, c, w_ada, b_ada, g_ffn1, w_ffn1_gu, w_ffn1_down, g_mix, w_in, pool_mix,
           pool_scale, w_pool_up, q_gain, k_gain, sinks, rel_bias, w_attn_up, w_o,
           g_ffn2, w_ffn2_gu, w_ffn2_down):
    b, s, d = x.shape
    depth = w_ada.shape[0]
    assert b == 1, "adaLN matvec and row tiling assume a single sequence"
    xs = x[0]
    for l in range(depth):
        mod = _ada(c.reshape(d, 1), w_ada[l:l + 1], b_ada[l:l + 1], tn=1024)
        mod = mod.reshape(3 * N_SUB, d)

        later = (w_ffn2_gu, w_ffn2_down, w_in, pool_mix.reshape(depth, POOL_W, POOL_GROUP_W),
                 w_pool_up, w_attn_up, w_o)
        xs, (gu2, down2, w_in_b, pmix_b, wpool_b, wattn_b, wo_b) = _ffn(
            xs, mod, g_ffn1[l:l + 1], w_ffn1_gu, w_ffn1_down, later,
            layer=l, sub=0, tm=1024, tf=256, x_buffers=1, row_parts=2)

        z = _inproj(xs, mod, g_mix[l:l + 1], w_in_b, sub=1, tm=1024, tn=1280, nchunk=4)
        attn = _attention(z, q_gain[l], k_gain[l], rel_bias, sinks[l], tq=512)
        xs = _mix(z, attn, xs, mod, pmix_b.reshape(pool_mix.shape[1:]), pool_scale[l:l + 1],
                  wpool_b, wattn_b, wo_b, sub=1, tm=256)

        xs = _ffn_pipe(xs, mod, g_ffn2[l:l + 1], gu2, down2, sub=2, tm=1024, tf=512,
                       nchunk=8, row_parts=2)
    return xs[None]
```

```python
import functools

import jax
import jax.numpy as jnp
import numpy as np
from jax import lax
from jax.experimental import pallas as pl
from jax.experimental.pallas import tpu as pltpu

F32 = jnp.float32
BF16 = jnp.bfloat16

N_SUB = 3
POOL_WINDOWS = (2, 4, 8, 16)
POOL_GROUP_W = 256
POOL_W = len(POOL_WINDOWS) * POOL_GROUP_W
HEAD_DIM = 64
N_HEADS = 16
N_KV_HEADS = 2
GQA_GROUP = N_HEADS // N_KV_HEADS
ATTN_W = N_HEADS * HEAD_DIM
KV_W = 2 * N_KV_HEADS * HEAD_DIM
BLK = 128
NUM_BUCKETS = 32
MAX_EXACT = NUM_BUCKETS // 2
REL_MAX_DIST = 128
EPS = 1e-6
NEG_INF = -1e30
LOG2E = float(np.log2(np.e))
POOL_HALO = 16

LANES = 128
BF16_SUBLANES = 16
MXU_COLS = 256

V7X_VMEM_BYTES = 64 * 1024 * 1024
VMEM_LIMIT = V7X_VMEM_BYTES - 6 * 1024 * 1024


def _rel_bucket_band():
    ql = np.arange(BLK)[:, None]
    j = np.arange(2 * BLK)[None, :]
    n = np.clip(BLK + ql - j, 0, None)
    nf = np.maximum(n, 1).astype(np.float32)
    large = MAX_EXACT + (np.log(nf / MAX_EXACT) / np.log(REL_MAX_DIST / MAX_EXACT)
                         * (NUM_BUCKETS - MAX_EXACT)).astype(np.int32)
    large = np.minimum(large, NUM_BUCKETS - 1)
    return np.where(n < MAX_EXACT, n, large).astype(np.int32)


def _norm_modulate(x, gain, shift, scale):
    ms = jnp.mean(x * x, axis=-1, keepdims=True)
    y = x * lax.rsqrt(ms + EPS) * gain
    return y * (1.0 + scale) + shift


def _ada_kernel(c_ref, w_ref, b_ref, o_ref, sc_ref):
    cc = c_ref[...]
    sc = cc * jax.nn.sigmoid(cc)
    sc_ref[...] = sc
    o_ref[...] = jnp.sum(w_ref[...] * sc, axis=0, keepdims=True) + b_ref[...]


def _ada(c_col, w_ada, b_ada, *, layer, tn, n_cols):
    d = w_ada.shape[1]
    return pl.pallas_call(
        _ada_kernel,
        grid=(n_cols // tn,),
        in_specs=[
            pl.BlockSpec((d, 1), lambda j: (0, 0)),
            pl.BlockSpec((None, d, tn), lambda j: (layer, 0, j)),
            pl.BlockSpec((None, 1, tn), lambda j: (layer, 0, j)),
        ],
        out_specs=[pl.BlockSpec((1, tn), lambda j: (0, j)),
                   pl.BlockSpec((d, 1), lambda j: (0, 0))],
        out_shape=[jax.ShapeDtypeStruct((1, n_cols), F32), jax.ShapeDtypeStruct((d, 1), F32)],
        compiler_params=pltpu.CompilerParams(
            dimension_semantics=("arbitrary",), vmem_limit_bytes=VMEM_LIMIT),
        name="ada_ln",
    )(c_col, w_ada, b_ada.reshape(b_ada.shape[0], 1, -1))


def _ffn_kernel(x_ref, mod_ref, g_ref, wg_ref, wu_ref, wd_ref, sc_ref, wada_ref, bada_ref,
                *rest, sub, n_side, row_parts):
    side_in, o_ref, ada_ref, side_out, h_ref = (
        rest[:n_side], rest[n_side], rest[n_side + 1], rest[n_side + 2:2 * n_side + 2],
        rest[-1])
    j = pl.program_id(1)

    @pl.when(j == 0)
    def _():
        h = _norm_modulate(x_ref[...], g_ref[...],
                           mod_ref[3 * sub:3 * sub + 1, :],
                           mod_ref[3 * sub + 1:3 * sub + 2, :])
        h_ref[...] = h.astype(BF16)
        o_ref[...] = jnp.zeros_like(o_ref)

    wg, wu, wd = (w[...].astype(BF16) for w in (wg_ref, wu_ref, wd_ref))
    pm = o_ref.shape[0] // row_parts
    for r in range(row_parts):
        rows = slice(r * pm, (r + 1) * pm)
        h = h_ref[rows, :]
        g = jnp.dot(h, wg, preferred_element_type=F32)
        u = jnp.dot(h, wu, preferred_element_type=F32)
        a = (g * jax.nn.sigmoid(g) * u).astype(BF16)
        o_ref[rows, :] += jnp.dot(a, wd, preferred_element_type=F32)

    for src, dst in zip(side_in, side_out):
        dst[...] = src[...].astype(BF16)
    ada_ref[...] = jnp.sum(wada_ref[...] * sc_ref[...], axis=0, keepdims=True) + bada_ref[...]

    @pl.when(j == pl.num_programs(1) - 1)
    def _():
        gate = mod_ref[3 * sub + 2:3 * sub + 3, :]
        o_ref[...] = x_ref[...] + (0.5 * gate) * o_ref[...]


def _next_tile_chunk_index(ni, nchunk):
    def index(i, j):
        return jnp.minimum(i + 1, ni - 1) * nchunk + jnp.minimum(j, nchunk - 1), 0
    return index


def _ffn_pipe_kernel(xn_ref, x_hbm, mod_ref, g_ref, wg_ref, wu_ref, wd_ref, o_ref,
                     h_even, h_odd, xs_scr, *, sub, nchunk, row_parts):
    i, j = pl.program_id(0), pl.program_id(1)
    tm = o_ref.shape[0]
    shift = mod_ref[3 * sub:3 * sub + 1, :]
    scale = mod_ref[3 * sub + 1:3 * sub + 2, :]
    half_gate = 0.5 * mod_ref[3 * sub + 2:3 * sub + 3, :]

    @pl.when(j == 0)
    def _():
        @pl.when(i == 0)
        def _():
            pltpu.sync_copy(x_hbm.at[pl.ds(0, tm), :], xs_scr)
            h_even[...] = _norm_modulate(xs_scr[...], g_ref[...], shift, scale).astype(BF16)

        o_ref[...] = xs_scr[...]

    def step(h_cur, h_next):
        rc = xn_ref.shape[0]
        r0 = pl.multiple_of(jnp.minimum(j, nchunk - 1) * rc, rc)
        xn = xn_ref[...]
        h_next[pl.ds(r0, rc), :] = _norm_modulate(xn, g_ref[...], shift, scale).astype(BF16)
        xs_scr[pl.ds(r0, rc), :] = xn

        pm = tm // row_parts
        for r in range(row_parts):
            rows = slice(r * pm, (r + 1) * pm)
            h = h_cur[rows, :]
            g = jnp.dot(h, wg_ref[...], preferred_element_type=F32)
            u = jnp.dot(h, wu_ref[...], preferred_element_type=F32)
            a = (g * jax.nn.sigmoid(g) * u).astype(BF16)
            o_ref[rows, :] += jnp.dot(a, wd_ref[...], preferred_element_type=F32) * half_gate

    pl.when(i % 2 == 0)(lambda: step(h_even, h_odd))
    pl.when(i % 2 == 1)(lambda: step(h_odd, h_even))


def _ffn_pipe(x, mod, gain, w_gu, w_down, *, sub, tm, tf, nchunk, row_parts):
    s, d = x.shape
    dff = w_down.shape[0]
    ni, nf = s // tm, dff // tf
    assert nchunk <= nf and tm % nchunk == 0 and tm % row_parts == 0
    return pl.pallas_call(
        functools.partial(_ffn_pipe_kernel, sub=sub, nchunk=nchunk, row_parts=row_parts),
        grid=(ni, nf),
        in_specs=[
            pl.BlockSpec((tm // nchunk, d), _next_tile_chunk_index(ni, nchunk)),
            pl.BlockSpec(memory_space=pl.ANY),
            pl.BlockSpec(mod.shape, lambda i, j: (0, 0)),
            pl.BlockSpec((1, d), lambda i, j: (0, 0)),
            pl.BlockSpec((d, tf), lambda i, j: (0, j)),
            pl.BlockSpec((d, tf), lambda i, j: (0, j + nf)),
            pl.BlockSpec((tf, d), lambda i, j: (j, 0)),
        ],
        out_specs=pl.BlockSpec((tm, d), lambda i, j: (i, 0)),
        out_shape=jax.ShapeDtypeStruct((s, d), F32),
        scratch_shapes=[pltpu.VMEM((tm, d), BF16), pltpu.VMEM((tm, d), BF16),
                        pltpu.VMEM((tm, d), F32)],
        compiler_params=pltpu.CompilerParams(
            dimension_semantics=("arbitrary", "arbitrary"), vmem_limit_bytes=VMEM_LIMIT),
        name=f"ffn{sub}",
    )(x, x, mod, gain, w_gu, w_gu, w_down)


def _side_cast_blocking(rows, cols, n_steps):
    best = None
    for bc in range(LANES, cols + 1, LANES):
        for br in range(BF16_SUBLANES, rows + 1, BF16_SUBLANES):
            if cols % bc or rows % br or (rows // br) * (cols // bc) > n_steps:
                continue
            if best is None or br * bc < best[0] * best[1]:
                best = (br, bc)
    assert best is not None, (rows, cols, n_steps)
    return best


def _side_cast_specs(side, layer, n_steps, flat_step):
    specs_in, specs_out, shapes = [], [], []
    for w in side:
        rows, cols = w.shape[1:]
        br, bc = _side_cast_blocking(rows, cols, n_steps)
        ncb, last = cols // bc, (rows // br) * (cols // bc) - 1

        def block_index(*idx, ncb=ncb, last=last):
            t = jnp.minimum(flat_step(*idx), last)
            return t // ncb, t % ncb

        specs_in.append(pl.BlockSpec(
            (None, br, bc), lambda *idx, f=block_index: (layer, *f(*idx))))
        specs_out.append(pl.BlockSpec((br, bc), block_index))
        shapes.append(jax.ShapeDtypeStruct((rows, cols), BF16))
    return specs_in, specs_out, shapes


def _ffn(x, mod, gain, w_gu, w_down, side, sc_col, w_ada, b_ada, *, layer, sub, tm, tf,
         x_buffers, row_parts, ada_from):
    s, d = x.shape
    dff = w_down.shape[1]
    ni, nf = s // tm, dff // tf
    flat_step = lambda i, j: i * nf + j
    side_in, side_out, side_shapes = _side_cast_specs(side, layer, ni * nf, flat_step)

    n_ada = w_ada.shape[2] - ada_from
    ada_blocks = n_ada // LANES
    assert ada_from % LANES == 0 and n_ada % LANES == 0 and ada_blocks <= ni * nf

    def ada_step(i, j):
        return jnp.minimum(flat_step(i, j), ada_blocks - 1)

    x_mode = {} if x_buffers == 2 else dict(pipeline_mode=pl.Buffered(x_buffers))
    outs = pl.pallas_call(
        functools.partial(_ffn_kernel, sub=sub, n_side=len(side), row_parts=row_parts),
        grid=(ni, nf),
        in_specs=[
            pl.BlockSpec((tm, d), lambda i, j: (i, 0), **x_mode),
            pl.BlockSpec(mod.shape, lambda i, j: (0, 0)),
            pl.BlockSpec((1, d), lambda i, j: (0, 0)),
            pl.BlockSpec((None, d, tf), lambda i, j: (layer, 0, j)),
            pl.BlockSpec((None, d, tf), lambda i, j: (layer, 0, j + nf)),
            pl.BlockSpec((None, tf, d), lambda i, j: (layer, j, 0)),
            pl.BlockSpec((d, 1), lambda i, j: (0, 0), pipeline_mode=pl.Buffered(1)),
            pl.BlockSpec((None, d, LANES),
                         lambda i, j: (layer, 0, ada_from // LANES + ada_step(i, j))),
            pl.BlockSpec((None, 1, LANES),
                         lambda i, j: (layer, 0, ada_from // LANES + ada_step(i, j))),
            *side_in,
        ],
        out_specs=[pl.BlockSpec((tm, d), lambda i, j: (i, 0)),
                   pl.BlockSpec((1, LANES), lambda i, j: (0, ada_step(i, j))),
                   *side_out],
        out_shape=[jax.ShapeDtypeStruct((s, d), F32),
                   jax.ShapeDtypeStruct((1, n_ada), F32), *side_shapes],
        scratch_shapes=[pltpu.VMEM((tm, d), BF16)],
        compiler_params=pltpu.CompilerParams(
            dimension_semantics=("arbitrary", "arbitrary"), vmem_limit_bytes=VMEM_LIMIT),
        name=f"ffn{sub}",
    )(x, mod, gain, w_gu, w_gu, w_down, sc_col, w_ada,
      b_ada.reshape(b_ada.shape[0], 1, -1), *side)
    return outs[0], outs[1], outs[2:]


def _inproj_kernel(xn_ref, x0_ref, mod_ref, g_ref, w_ref, *rest, sub, nchunk, n_side):
    side_in, o_ref, side_out, h_even, h_odd = (
        rest[:n_side], rest[n_side], rest[n_side + 1:2 * n_side + 1], rest[-2], rest[-1])
    i, j = pl.program_id(0), pl.program_id(1)
    shift = mod_ref[3 * sub:3 * sub + 1, :]
    scale = mod_ref[3 * sub + 1:3 * sub + 2, :]

    @pl.when(jnp.logical_and(i == 0, j == 0))
    def _():
        h_even[...] = _norm_modulate(x0_ref[...], g_ref[...], shift, scale).astype(BF16)

    def step(h_cur, h_next):
        rc = xn_ref.shape[0]
        r0 = pl.multiple_of(jnp.minimum(j, nchunk - 1) * rc, rc)
        h_next[pl.ds(r0, rc), :] = _norm_modulate(
            xn_ref[...], g_ref[...], shift, scale).astype(BF16)
        for src, dst in zip(side_in, side_out):
            dst[...] = src[...].astype(BF16)
        o_ref[...] = jnp.dot(h_cur[...], w_ref[...], preferred_element_type=F32)

    pl.when(i % 2 == 0)(lambda: step(h_even, h_odd))
    pl.when(i % 2 == 1)(lambda: step(h_odd, h_even))


def _inproj(x, mod, gain, w, side, *, layer, sub, tm, tn, nchunk):
    s, d = x.shape
    n = w.shape[1]
    ni, nj = s // tm, n // tn
    assert nchunk <= nj and tm % nchunk == 0
    side_in, side_out, side_shapes = _side_cast_specs(
        side, layer, ni * nj, lambda i, j: i * nj + j)
    outs = pl.pallas_call(
        functools.partial(_inproj_kernel, sub=sub, nchunk=nchunk, n_side=len(side)),
        grid=(ni, nj),
        in_specs=[
            pl.BlockSpec((tm // nchunk, d), _next_tile_chunk_index(ni, nchunk)),
            pl.BlockSpec((tm, d), lambda i, j: (0, 0), pipeline_mode=pl.Buffered(1)),
            pl.BlockSpec(mod.shape, lambda i, j: (0, 0)),
            pl.BlockSpec((1, d), lambda i, j: (0, 0)),
            pl.BlockSpec((d, tn), lambda i, j: (0, j)),
            *side_in,
        ],
        out_specs=[pl.BlockSpec((tm, tn), lambda i, j: (i, j)), *side_out],
        out_shape=[jax.ShapeDtypeStruct((s, n), F32), *side_shapes],
        scratch_shapes=[pltpu.VMEM((tm, d), BF16), pltpu.VMEM((tm, d), BF16)],
        compiler_params=pltpu.CompilerParams(
            dimension_semantics=("arbitrary", "arbitrary"), vmem_limit_bytes=VMEM_LIMIT),
        name="in_proj",
    )(x, x, mod, gain, w, *side)
    return outs[0], outs[1:]


def _rms_over_rows(t):
    ms = jnp.mean(t * t, axis=1, keepdims=True)
    return t * lax.rsqrt(ms + EPS)


def _attn_kernel(q_ref, kv_ref, kvp_ref, qg_ref, kg_ref, bucket_ref, rel_ref, sink_ref,
                 o_ref, bias_scr, sink_scr, *, nblk):
    i = pl.program_id(0)

    from_prev = (lax.broadcasted_iota(jnp.int32, (BLK, BLK), 0)
                 > lax.broadcasted_iota(jnp.int32, (BLK, BLK), 1))

    @pl.when(i == 0)
    def _():
        bucket = bucket_ref[...]
        for h in range(N_HEADS):
            kh, g = divmod(h, GQA_GROUP)
            lanes = slice(g * BLK, (g + 1) * BLK)
            acc = jnp.zeros((BLK, BLK), F32)
            for b in range(NUM_BUCKETS):
                acc = jnp.where(bucket == b, rel_ref[b, h] * LOG2E, acc)
            bias_scr[kh, :, lanes] = acc
            bias_scr[N_KV_HEADS + kh, :, lanes] = jnp.where(from_prev, NEG_INF, acc)
            sink_scr[kh, :, lanes] = jnp.full((1, BLK), sink_ref[h] * LOG2E, F32)

    t = (nblk + 1) * BLK
    kv_t = jnp.concatenate([kvp_ref[...], kv_ref[...]], axis=0).T
    k_t = kv_t[0:N_KV_HEADS * HEAD_DIM].reshape(N_KV_HEADS, HEAD_DIM, t)
    k_gain = kg_ref[...] * qg_ref[...] * (HEAD_DIM ** -0.5 * LOG2E)
    kn = (_rms_over_rows(k_t) * k_gain).reshape(N_KV_HEADS * HEAD_DIM, t).T
    lane = lax.broadcasted_iota(jnp.int32, kn.shape, 1)
    k_sel = [jnp.where((lane // HEAD_DIM) == kh, kn, 0.0).astype(BF16)
             for kh in range(N_KV_HEADS)]
    v_t = [kv_t[(N_KV_HEADS + kh) * HEAD_DIM:(N_KV_HEADS + kh + 1) * HEAD_DIM].astype(BF16)
           for kh in range(N_KV_HEADS)]

    for b in range(nblk):
        rows = slice(b * BLK, (b + 1) * BLK)
        band = slice(b * BLK, (b + 2) * BLK)
        q_t = q_ref[rows, :].T.reshape(N_HEADS, HEAD_DIM, BLK)
        qn = _rms_over_rows(q_t).astype(BF16)
        out_t = []
        for kh in range(N_KV_HEADS):
            q_grp = jnp.concatenate([qn[kh * GQA_GROUP + g] for g in range(GQA_GROUP)],
                                    axis=1)
            q_dup = jnp.concatenate([q_grp] * N_KV_HEADS, axis=0)
            s = jnp.dot(k_sel[kh][band], q_dup, preferred_element_type=F32)
            table = kh + N_KV_HEADS * (i == 0).astype(jnp.int32) if b == 0 else kh
            e_parts, denom_parts = [], []
            for g in range(GQA_GROUP):
                lanes = slice(g * BLK, (g + 1) * BLK)
                sh = jnp.where(from_prev, s[0:BLK, lanes], s[BLK:2 * BLK, lanes])
                sh = sh + bias_scr[table, :, lanes]
                sink = sink_scr[kh, :, lanes]
                m = jnp.maximum(jnp.max(sh, axis=0, keepdims=True), sink)
                e = jnp.exp2(sh - m)
                denom_parts.append(jnp.sum(e, axis=0, keepdims=True) + jnp.exp2(sink - m))
                e_parts.append(jnp.concatenate(
                    [jnp.where(from_prev, e, 0.0), jnp.where(from_prev, 0.0, e)],
                    axis=0).astype(BF16))
            e = jnp.concatenate(e_parts, axis=1)
            denom = jnp.concatenate(denom_parts, axis=1)
            o = jnp.dot(v_t[kh][:, band], e, preferred_element_type=F32)
            o = o * (1.0 / denom)
            out_t += [o[:, g * BLK:(g + 1) * BLK] for g in range(GQA_GROUP)]
        o_ref[rows, :] = jnp.concatenate(out_t, axis=0).T.astype(BF16)


def _attention(z, q_gain, k_gain, rel_bias, sinks, *, tq):
    s = z.shape[0]
    nblk = tq // BLK
    gw = GQA_GROUP * BLK
    q_col = POOL_W // ATTN_W
    kv_col = (POOL_W + ATTN_W) // KV_W
    bucket_t = _rel_bucket_band().T
    jm, ql = np.meshgrid(np.arange(BLK), np.arange(BLK), indexing="ij")
    bucket_fold = jnp.asarray(np.where(jm > ql, bucket_t[:BLK], bucket_t[BLK:]))
    return pl.pallas_call(
        functools.partial(_attn_kernel, nblk=nblk),
        grid=(s // tq,),
        in_specs=[
            pl.BlockSpec((tq, ATTN_W), lambda i: (i, q_col)),
            pl.BlockSpec((tq, KV_W), lambda i: (i, kv_col)),
            pl.BlockSpec((BLK, KV_W), lambda i: (jnp.maximum(i * nblk - 1, 0), kv_col)),
            pl.BlockSpec((HEAD_DIM, 1), lambda i: (0, 0)),
            pl.BlockSpec((HEAD_DIM, 1), lambda i: (0, 0)),
            pl.BlockSpec((BLK, BLK), lambda i: (0, 0)),
            pl.BlockSpec(memory_space=pltpu.SMEM),
            pl.BlockSpec(memory_space=pltpu.SMEM),
        ],
        out_specs=pl.BlockSpec((tq, ATTN_W), lambda i: (i, 0)),
        out_shape=jax.ShapeDtypeStruct((s, ATTN_W), BF16),
        scratch_shapes=[
            pltpu.VMEM((2 * N_KV_HEADS, BLK, gw), F32),
            pltpu.VMEM((N_KV_HEADS, 1, gw), F32),
        ],
        compiler_params=pltpu.CompilerParams(
            dimension_semantics=("arbitrary",), vmem_limit_bytes=VMEM_LIMIT),
        name="swa_attention",
    )(z, z, z, q_gain.reshape(HEAD_DIM, 1), k_gain.reshape(HEAD_DIM, 1), bucket_fold,
      rel_bias, sinks)


def _mix_kernel(u_ref, up_ref, ga_ref, gb_ref, attn_ref, x_ref, mod_ref, pmix_ref,
                pscale_ref, wpool_ref, wattn_ref, wo_ref, o_ref, ubuf, *, sub, tm):
    i = pl.program_id(0)
    d = o_ref.shape[1]
    halo = up_ref[...]
    ubuf[0:POOL_HALO, :] = jnp.where(i == 0, 0.0, halo)
    ubuf[POOL_HALO:, :] = u_ref[...]

    t1 = i * tm + lax.broadcasted_iota(jnp.int32, (tm, POOL_GROUP_W), 0) + 1
    mixed, merged_attn = [], []
    dc = d // len(POOL_WINDOWS)
    for gi, w in enumerate(POOL_WINDOWS):
        out_cols = slice(gi * dc, (gi + 1) * dc)
        y_attn = jnp.dot(attn_ref[...], wattn_ref[:, out_cols], preferred_element_type=F32)
        merged_attn.append(jax.nn.sigmoid(gb_ref[:, out_cols]) * y_attn)

        cols = slice(gi * POOL_GROUP_W, (gi + 1) * POOL_GROUP_W)
        cur = ubuf[POOL_HALO:POOL_HALO + tm, cols]
        win = cur
        for dlt in range(1, w):
            win = win + ubuf[POOL_HALO - dlt:POOL_HALO - dlt + tm, cols]
        cnt = jnp.minimum(t1, w).astype(F32)
        pooled = (win / cnt - cur).astype(BF16)
        mg = jnp.dot(pooled, pmix_ref[gi], preferred_element_type=F32)
        mixed.append((mg * pscale_ref[:, cols]).astype(BF16))
    mixed = jnp.concatenate(mixed, axis=1)

    y_pool = jnp.dot(mixed, wpool_ref[...], preferred_element_type=F32)
    merged = jax.nn.sigmoid(ga_ref[...]) * y_pool + jnp.concatenate(merged_attn, axis=1)
    y = jnp.dot(merged.astype(BF16), wo_ref[...], preferred_element_type=F32)
    gate = mod_ref[3 * sub + 2:3 * sub + 3, :]
    o_ref[...] = x_ref[...] + gate * y


def _mix(z, attn, x, mod, pool_mix, pool_scale, w_pool_up, w_attn_up, w_o, *, sub, tm):
    s, d = x.shape
    const = dict(pipeline_mode=pl.Buffered(1))
    ga_off = POOL_W + ATTN_W + KV_W
    return pl.pallas_call(
        functools.partial(_mix_kernel, sub=sub, tm=tm),
        grid=(s // tm,),
        in_specs=[
            pl.BlockSpec((tm, POOL_W), lambda i: (i, 0)),
            pl.BlockSpec((POOL_HALO, POOL_W),
                         lambda i: (jnp.maximum(i * (tm // POOL_HALO) - 1, 0), 0)),
            pl.BlockSpec((pl.Element(tm), pl.Element(d)), lambda i: (i * tm, ga_off)),
            pl.BlockSpec((pl.Element(tm), pl.Element(d)), lambda i: (i * tm, ga_off + d)),
            pl.BlockSpec((tm, ATTN_W), lambda i: (i, 0)),
            pl.BlockSpec((tm, d), lambda i: (i, 0)),
            pl.BlockSpec(mod.shape, lambda i: (0, 0), **const),
            pl.BlockSpec(pool_mix.shape, lambda i: (0, 0, 0), **const),
            pl.BlockSpec((1, POOL_W), lambda i: (0, 0), **const),
            pl.BlockSpec(w_pool_up.shape, lambda i: (0, 0), **const),
            pl.BlockSpec(w_attn_up.shape, lambda i: (0, 0), **const),
            pl.BlockSpec(w_o.shape, lambda i: (0, 0), **const),
        ],
        out_specs=pl.BlockSpec((tm, d), lambda i: (i, 0)),
        out_shape=jax.ShapeDtypeStruct((s, d), F32),
        scratch_shapes=[pltpu.VMEM((tm + POOL_HALO, POOL_W), F32)],
        compiler_params=pltpu.CompilerParams(
            dimension_semantics=("arbitrary",), vmem_limit_bytes=VMEM_LIMIT),
        name="pool_merge_out",
    )(z, z, z, z, attn, x, mod, pool_mix, pool_scale, w_pool_up, w_attn_up, w_o)


def kernel(x, c, w_ada, b_ada, g_ffn1, w_ffn1_gu, w_ffn1_down, g_mix, w_in, pool_mix,
           pool_scale, w_pool_up, q_gain, k_gain, sinks, rel_bias, w_attn_up, w_o,
           g_ffn2, w_ffn2_gu, w_ffn2_down):
    b, s, d = x.shape
    depth = w_ada.shape[0]
    assert b == 1, "adaLN matvec and row tiling assume a single sequence"
    xs = x[0]
    for l in range(depth):
        mod_1, sc_col = _ada(c.reshape(d, 1), w_ada, b_ada, layer=l, tn=1024, n_cols=3 * d)

        xs, mod_rest, (gu2, down2, w_in_b) = _ffn(
            xs, mod_1.reshape(3, d), g_ffn1[l:l + 1], w_ffn1_gu, w_ffn1_down,
            (w_ffn2_gu, w_ffn2_down, w_in), sc_col, w_ada, b_ada,
            layer=l, sub=0, tm=1024, tf=256, x_buffers=1, row_parts=2, ada_from=3 * d)
        mod = jnp.concatenate([mod_1, mod_rest], axis=1).reshape(3 * N_SUB, d)

        z, (pmix_b, wpool_b, wattn_b, wo_b) = _inproj(
            xs, mod, g_mix[l:l + 1], w_in_b,
            (pool_mix.reshape(depth, POOL_W, POOL_GROUP_W), w_pool_up, w_attn_up, w_o),
            layer=l, sub=1, tm=1024, tn=1280, nchunk=4)
        attn = _attention(z, q_gain[l], k_gain[l], rel_bias, sinks[l], tq=512)
        xs = _mix(z, attn, xs, mod, pmix_b.reshape(pool_mix.shape[1:]), pool_scale[l:l + 1],
                  wpool_b, wattn_b, wo_b, sub=1, tm=256)

        xs = _ffn_pipe(xs, mod, g_ffn2[l:l + 1], gu2, down2, sub=2, tm=1024, tf=512,
                       nchunk=8, row_parts=2)
    return xs[None]
```

```python
import functools

import jax
import jax.numpy as jnp
import numpy as np
from jax import lax
from jax.experimental import pallas as pl
from jax.experimental.pallas import tpu as pltpu

F32 = jnp.float32
BF16 = jnp.bfloat16

N_SUB = 3
POOL_WINDOWS = (2, 4, 8, 16)
POOL_GROUP_W = 256
POOL_W = len(POOL_WINDOWS) * POOL_GROUP_W
HEAD_DIM = 64
N_HEADS = 16
N_KV_HEADS = 2
GQA_GROUP = N_HEADS // N_KV_HEADS
ATTN_W = N_HEADS * HEAD_DIM
KV_W = 2 * N_KV_HEADS * HEAD_DIM
BLK = 128
NUM_BUCKETS = 32
MAX_EXACT = NUM_BUCKETS // 2
REL_MAX_DIST = 128
EPS = 1e-6
NEG_INF = -1e30
LOG2E = float(np.log2(np.e))
POOL_HALO = 16

LANES = 128
BF16_SUBLANES = 16
MXU_COLS = 256

V7X_VMEM_BYTES = 64 * 1024 * 1024
VMEM_LIMIT = V7X_VMEM_BYTES - 6 * 1024 * 1024


def _rel_bucket_band():
    ql = np.arange(BLK)[:, None]
    j = np.arange(2 * BLK)[None, :]
    n = np.clip(BLK + ql - j, 0, None)
    nf = np.maximum(n, 1).astype(np.float32)
    large = MAX_EXACT + (np.log(nf / MAX_EXACT) / np.log(REL_MAX_DIST / MAX_EXACT)
                         * (NUM_BUCKETS - MAX_EXACT)).astype(np.int32)
    large = np.minimum(large, NUM_BUCKETS - 1)
    return np.where(n < MAX_EXACT, n, large).astype(np.int32)


def _norm_modulate(x, gain, shift, scale):
    ms = jnp.mean(x * x, axis=-1, keepdims=True)
    y = x * lax.rsqrt(ms + EPS) * gain
    return y * (1.0 + scale) + shift


def _ada_kernel(c_ref, w_ref, b_ref, o_ref, sc_ref):
    cc = c_ref[...]
    sc = cc * jax.nn.sigmoid(cc)
    sc_ref[...] = sc
    o_ref[...] = jnp.sum(w_ref[...] * sc, axis=0, keepdims=True) + b_ref[...]


def _ada(c_col, w_ada, b_ada, *, layer, tn, n_cols):
    d = w_ada.shape[1]
    return pl.pallas_call(
        _ada_kernel,
        grid=(n_cols // tn,),
        in_specs=[
            pl.BlockSpec((d, 1), lambda j: (0, 0)),
            pl.BlockSpec((None, d, tn), lambda j: (layer, 0, j)),
            pl.BlockSpec((None, 1, tn), lambda j: (layer, 0, j)),
        ],
        out_specs=[pl.BlockSpec((1, tn), lambda j: (0, j)),
                   pl.BlockSpec((d, 1), lambda j: (0, 0))],
        out_shape=[jax.ShapeDtypeStruct((1, n_cols), F32), jax.ShapeDtypeStruct((d, 1), F32)],
        compiler_params=pltpu.CompilerParams(
            dimension_semantics=("arbitrary",), vmem_limit_bytes=VMEM_LIMIT),
        name="ada_ln",
    )(c_col, w_ada, b_ada.reshape(b_ada.shape[0], 1, -1))


def _norm_rows_kernel(x_ref, mod_ref, g_ref, o_ref, *, sub):
    o_ref[...] = _norm_modulate(x_ref[...], g_ref[...],
                                mod_ref[3 * sub:3 * sub + 1, :],
                                mod_ref[3 * sub + 1:3 * sub + 2, :]).astype(BF16)


def _ffn_up_kernel(h_ref, wg_ref, wu_ref, sc_ref, wada_ref, bada_ref, *rest, n_side,
                   row_parts):
    side_in, a_ref, ada_ref, side_out, wg_s, wu_s = (
        rest[:n_side], rest[n_side], rest[n_side + 1], rest[n_side + 2:2 * n_side + 2],
        rest[-2], rest[-1])

    @pl.when(pl.program_id(1) == 0)
    def _():
        wg_s[...] = wg_ref[...].astype(BF16)
        wu_s[...] = wu_ref[...].astype(BF16)

    pm = a_ref.shape[0] // row_parts
    for r in range(row_parts):
        rows = slice(r * pm, (r + 1) * pm)
        h = h_ref[rows, :]
        g = jnp.dot(h, wg_s[...], preferred_element_type=F32)
        u = jnp.dot(h, wu_s[...], preferred_element_type=F32)
        a_ref[rows, :] = (g * jax.nn.sigmoid(g) * u).astype(BF16)

    for src, dst in zip(side_in, side_out):
        dst[...] = src[...].astype(BF16)
    ada_ref[...] = jnp.sum(wada_ref[...] * sc_ref[...], axis=0, keepdims=True) + bada_ref[...]


def _ffn_down_kernel(a_ref, wd_ref, x_ref, mod_ref, o_ref, *, sub):
    y = jnp.dot(a_ref[...], wd_ref[...], preferred_element_type=F32)
    o_ref[...] = x_ref[...] + (0.5 * mod_ref[3 * sub + 2:3 * sub + 3, :]) * y


def _next_tile_chunk_index(ni, nchunk):
    def index(i, j):
        return jnp.minimum(i + 1, ni - 1) * nchunk + jnp.minimum(j, nchunk - 1), 0
    return index


def _ffn_pipe_kernel(xn_ref, x_hbm, mod_ref, g_ref, wg_ref, wu_ref, wd_ref, o_ref,
                     h_even, h_odd, xs_scr, *, sub, nchunk, row_parts):
    i, j = pl.program_id(0), pl.program_id(1)
    tm = o_ref.shape[0]
    shift = mod_ref[3 * sub:3 * sub + 1, :]
    scale = mod_ref[3 * sub + 1:3 * sub + 2, :]
    half_gate = 0.5 * mod_ref[3 * sub + 2:3 * sub + 3, :]

    @pl.when(j == 0)
    def _():
        @pl.when(i == 0)
        def _():
            pltpu.sync_copy(x_hbm.at[pl.ds(0, tm), :], xs_scr)
            h_even[...] = _norm_modulate(xs_scr[...], g_ref[...], shift, scale).astype(BF16)

        o_ref[...] = xs_scr[...]

    def step(h_cur, h_next):
        rc = xn_ref.shape[0]
        r0 = pl.multiple_of(jnp.minimum(j, nchunk - 1) * rc, rc)
        xn = xn_ref[...]
        h_next[pl.ds(r0, rc), :] = _norm_modulate(xn, g_ref[...], shift, scale).astype(BF16)
        xs_scr[pl.ds(r0, rc), :] = xn

        pm = tm // row_parts
        for r in range(row_parts):
            rows = slice(r * pm, (r + 1) * pm)
            h = h_cur[rows, :]
            g = jnp.dot(h, wg_ref[...], preferred_element_type=F32)
            u = jnp.dot(h, wu_ref[...], preferred_element_type=F32)
            a = (g * jax.nn.sigmoid(g) * u).astype(BF16)
            o_ref[rows, :] += jnp.dot(a, wd_ref[...], preferred_element_type=F32) * half_gate

    pl.when(i % 2 == 0)(lambda: step(h_even, h_odd))
    pl.when(i % 2 == 1)(lambda: step(h_odd, h_even))


def _ffn_pipe(x, mod, gain, w_gu, w_down, *, sub, tm, tf, nchunk, row_parts):
    s, d = x.shape
    dff = w_down.shape[0]
    ni, nf = s // tm, dff // tf
    assert nchunk <= nf and tm % nchunk == 0 and tm % row_parts == 0
    return pl.pallas_call(
        functools.partial(_ffn_pipe_kernel, sub=sub, nchunk=nchunk, row_parts=row_parts),
        grid=(ni, nf),
        in_specs=[
            pl.BlockSpec((tm // nchunk, d), _next_tile_chunk_index(ni, nchunk)),
            pl.BlockSpec(memory_space=pl.ANY),
            pl.BlockSpec(mod.shape, lambda i, j: (0, 0)),
            pl.BlockSpec((1, d), lambda i, j: (0, 0)),
            pl.BlockSpec((d, tf), lambda i, j: (0, j)),
            pl.BlockSpec((d, tf), lambda i, j: (0, j + nf)),
            pl.BlockSpec((tf, d), lambda i, j: (j, 0)),
        ],
        out_specs=pl.BlockSpec((tm, d), lambda i, j: (i, 0)),
        out_shape=jax.ShapeDtypeStruct((s, d), F32),
        scratch_shapes=[pltpu.VMEM((tm, d), BF16), pltpu.VMEM((tm, d), BF16),
                        pltpu.VMEM((tm, d), F32)],
        compiler_params=pltpu.CompilerParams(
            dimension_semantics=("arbitrary", "arbitrary"), vmem_limit_bytes=VMEM_LIMIT),
        name=f"ffn{sub}",
    )(x, x, mod, gain, w_gu, w_gu, w_down)


def _side_cast_blocking(rows, cols, n_steps):
    best = None
    for bc in range(LANES, cols + 1, LANES):
        for br in range(BF16_SUBLANES, rows + 1, BF16_SUBLANES):
            if cols % bc or rows % br or (rows // br) * (cols // bc) > n_steps:
                continue
            if best is None or br * bc < best[0] * best[1]:
                best = (br, bc)
    assert best is not None, (rows, cols, n_steps)
    return best


def _side_cast_specs(side, layer, n_steps, flat_step):
    specs_in, specs_out, shapes = [], [], []
    for w in side:
        rows, cols = w.shape[1:]
        br, bc = _side_cast_blocking(rows, cols, n_steps)
        ncb, last = cols // bc, (rows // br) * (cols // bc) - 1

        def block_index(*idx, ncb=ncb, last=last):
            t = jnp.minimum(flat_step(*idx), last)
            return t // ncb, t % ncb

        specs_in.append(pl.BlockSpec(
            (None, br, bc), lambda *idx, f=block_index: (layer, *f(*idx))))
        specs_out.append(pl.BlockSpec((br, bc), block_index))
        shapes.append(jax.ShapeDtypeStruct((rows, cols), BF16))
    return specs_in, specs_out, shapes


def _norm_rows(x, mod, gain, *, sub, tm):
    s, d = x.shape
    return pl.pallas_call(
        functools.partial(_norm_rows_kernel, sub=sub),
        grid=(s // tm,),
        in_specs=[
            pl.BlockSpec((tm, d), lambda i: (i, 0)),
            pl.BlockSpec(mod.shape, lambda i: (0, 0)),
            pl.BlockSpec((1, d), lambda i: (0, 0)),
        ],
        out_specs=pl.BlockSpec((tm, d), lambda i: (i, 0)),
        out_shape=jax.ShapeDtypeStruct((s, d), BF16),
        compiler_params=pltpu.CompilerParams(
            dimension_semantics=("arbitrary",), vmem_limit_bytes=VMEM_LIMIT),
        name=f"norm{sub}",
    )(x, mod, gain)


def _ffn_up(h, w_gu, side, sc_col, w_ada, b_ada, *, layer, sub, tm, tf, row_parts, ada_from):
    s, d = h.shape
    dff = w_gu.shape[2] // 2
    nf, ni = dff // tf, s // tm
    flat_step = lambda f, i: f * ni + i
    side_in, side_out, side_shapes = _side_cast_specs(side, layer, nf * ni, flat_step)

    n_ada = w_ada.shape[2] - ada_from
    ada_bc = next(bc for bc in range(LANES, n_ada + 1, LANES)
                  if n_ada % bc == 0 and ada_from % bc == 0 and n_ada // bc <= nf * ni)
    ada_blocks, ada_first = n_ada // ada_bc, ada_from // ada_bc

    def ada_step(f, i):
        return jnp.minimum(flat_step(f, i), ada_blocks - 1)

    outs = pl.pallas_call(
        functools.partial(_ffn_up_kernel, n_side=len(side), row_parts=row_parts),
        grid=(nf, ni),
        in_specs=[
            pl.BlockSpec((tm, d), lambda f, i: (i, 0)),
            pl.BlockSpec((None, d, tf), lambda f, i: (layer, 0, f)),
            pl.BlockSpec((None, d, tf), lambda f, i: (layer, 0, f + nf)),
            pl.BlockSpec((d, 1), lambda f, i: (0, 0), pipeline_mode=pl.Buffered(1)),
            pl.BlockSpec((None, d, ada_bc),
                         lambda f, i: (layer, 0, ada_first + ada_step(f, i))),
            pl.BlockSpec((None, 1, ada_bc),
                         lambda f, i: (layer, 0, ada_first + ada_step(f, i))),
            *side_in,
        ],
        out_specs=[pl.BlockSpec((tm, tf), lambda f, i: (i, f)),
                   pl.BlockSpec((1, ada_bc), lambda f, i: (0, ada_step(f, i))),
                   *side_out],
        out_shape=[jax.ShapeDtypeStruct((s, dff), BF16),
                   jax.ShapeDtypeStruct((1, n_ada), F32), *side_shapes],
        scratch_shapes=[pltpu.VMEM((d, tf), BF16), pltpu.VMEM((d, tf), BF16)],
        compiler_params=pltpu.CompilerParams(
            dimension_semantics=("arbitrary", "arbitrary"), vmem_limit_bytes=VMEM_LIMIT),
        name=f"ffn{sub}_up",
    )(h, w_gu, w_gu, sc_col, w_ada, b_ada.reshape(b_ada.shape[0], 1, -1), *side)
    return outs[0], outs[1], outs[2:]


def _ffn_down(a, w_down, x, mod, *, sub, tm, tn):
    s, dff = a.shape
    d = w_down.shape[1]
    return pl.pallas_call(
        functools.partial(_ffn_down_kernel, sub=sub),
        grid=(s // tm, d // tn),
        in_specs=[
            pl.BlockSpec((tm, dff), lambda i, n: (i, 0)),
            pl.BlockSpec((dff, tn), lambda i, n: (0, n)),
            pl.BlockSpec((tm, tn), lambda i, n: (i, n)),
            pl.BlockSpec((mod.shape[0], tn), lambda i, n: (0, n)),
        ],
        out_specs=pl.BlockSpec((tm, tn), lambda i, n: (i, n)),
        out_shape=jax.ShapeDtypeStruct((s, d), F32),
        compiler_params=pltpu.CompilerParams(
            dimension_semantics=("arbitrary", "arbitrary"), vmem_limit_bytes=VMEM_LIMIT),
        name=f"ffn{sub}_down",
    )(a, w_down, x, mod)


def _inproj_kernel(xn_ref, x0_ref, mod_ref, g_ref, w_ref, *rest, sub, nchunk, n_side):
    side_in, o_ref, side_out, h_even, h_odd = (
        rest[:n_side], rest[n_side], rest[n_side + 1:2 * n_side + 1], rest[-2], rest[-1])
    i, j = pl.program_id(0), pl.program_id(1)
    shift = mod_ref[3 * sub:3 * sub + 1, :]
    scale = mod_ref[3 * sub + 1:3 * sub + 2, :]

    @pl.when(jnp.logical_and(i == 0, j == 0))
    def _():
        h_even[...] = _norm_modulate(x0_ref[...], g_ref[...], shift, scale).astype(BF16)

    def step(h_cur, h_next):
        rc = xn_ref.shape[0]
        r0 = pl.multiple_of(jnp.minimum(j, nchunk - 1) * rc, rc)
        h_next[pl.ds(r0, rc), :] = _norm_modulate(
            xn_ref[...], g_ref[...], shift, scale).astype(BF16)
        for src, dst in zip(side_in, side_out):
            dst[...] = src[...].astype(BF16)
        o_ref[...] = jnp.dot(h_cur[...], w_ref[...], preferred_element_type=F32)

    pl.when(i % 2 == 0)(lambda: step(h_even, h_odd))
    pl.when(i % 2 == 1)(lambda: step(h_odd, h_even))


def _inproj(x, mod, gain, w, side, *, layer, sub, tm, tn, nchunk):
    s, d = x.shape
    n = w.shape[1]
    ni, nj = s // tm, n // tn
    assert nchunk <= nj and tm % nchunk == 0
    side_in, side_out, side_shapes = _side_cast_specs(
        side, layer, ni * nj, lambda i, j: i * nj + j)
    outs = pl.pallas_call(
        functools.partial(_inproj_kernel, sub=sub, nchunk=nchunk, n_side=len(side)),
        grid=(ni, nj),
        in_specs=[
            pl.BlockSpec((tm // nchunk, d), _next_tile_chunk_index(ni, nchunk)),
            pl.BlockSpec((tm, d), lambda i, j: (0, 0), pipeline_mode=pl.Buffered(1)),
            pl.BlockSpec(mod.shape, lambda i, j: (0, 0)),
            pl.BlockSpec((1, d), lambda i, j: (0, 0)),
            pl.BlockSpec((d, tn), lambda i, j: (0, j)),
            *side_in,
        ],
        out_specs=[pl.BlockSpec((tm, tn), lambda i, j: (i, j)), *side_out],
        out_shape=[jax.ShapeDtypeStruct((s, n), F32), *side_shapes],
        scratch_shapes=[pltpu.VMEM((tm, d), BF16), pltpu.VMEM((tm, d), BF16)],
        compiler_params=pltpu.CompilerParams(
            dimension_semantics=("arbitrary", "arbitrary"), vmem_limit_bytes=VMEM_LIMIT),
        name="in_proj",
    )(x, x, mod, gain, w, *side)
    return outs[0], outs[1:]


def _rms_over_rows(t):
    ms = jnp.mean(t * t, axis=1, keepdims=True)
    return t * lax.rsqrt(ms + EPS)


def _attn_kernel(q_ref, kv_ref, kvp_ref, qg_ref, kg_ref, bucket_ref, rel_ref, sink_ref,
                 o_ref, bias_scr, sink_scr, *, nblk):
    i = pl.program_id(0)

    from_prev = (lax.broadcasted_iota(jnp.int32, (BLK, BLK), 0)
                 > lax.broadcasted_iota(jnp.int32, (BLK, BLK), 1))

    @pl.when(i == 0)
    def _():
        bucket = bucket_ref[...]
        for h in range(N_HEADS):
            kh, g = divmod(h, GQA_GROUP)
            lanes = slice(g * BLK, (g + 1) * BLK)
            acc = jnp.zeros((BLK, BLK), F32)
            for b in range(NUM_BUCKETS):
                acc = jnp.where(bucket == b, rel_ref[b, h] * LOG2E, acc)
            bias_scr[kh, :, lanes] = acc
            bias_scr[N_KV_HEADS + kh, :, lanes] = jnp.where(from_prev, NEG_INF, acc)
            sink_scr[kh, :, lanes] = jnp.full((1, BLK), sink_ref[h] * LOG2E, F32)

    t = (nblk + 1) * BLK
    kv_t = jnp.concatenate([kvp_ref[...], kv_ref[...]], axis=0).T
    k_t = kv_t[0:N_KV_HEADS * HEAD_DIM].reshape(N_KV_HEADS, HEAD_DIM, t)
    k_gain = kg_ref[...] * qg_ref[...] * (HEAD_DIM ** -0.5 * LOG2E)
    kn = (_rms_over_rows(k_t) * k_gain).reshape(N_KV_HEADS * HEAD_DIM, t).T
    lane = lax.broadcasted_iota(jnp.int32, kn.shape, 1)
    k_sel = [jnp.where((lane // HEAD_DIM) == kh, kn, 0.0).astype(BF16)
             for kh in range(N_KV_HEADS)]
    v_t = [kv_t[(N_KV_HEADS + kh) * HEAD_DIM:(N_KV_HEADS + kh + 1) * HEAD_DIM].astype(BF16)
           for kh in range(N_KV_HEADS)]

    for b in range(nblk):
        rows = slice(b * BLK, (b + 1) * BLK)
        band = slice(b * BLK, (b + 2) * BLK)
        q_t = q_ref[rows, :].T.reshape(N_HEADS, HEAD_DIM, BLK)
        qn = _rms_over_rows(q_t).astype(BF16)
        out_t = []
        for kh in range(N_KV_HEADS):
            q_grp = jnp.concatenate([qn[kh * GQA_GROUP + g] for g in range(GQA_GROUP)],
                                    axis=1)
            q_dup = jnp.concatenate([q_grp] * N_KV_HEADS, axis=0)
            s = jnp.dot(k_sel[kh][band], q_dup, preferred_element_type=F32)
            table = kh + N_KV_HEADS * (i == 0).astype(jnp.int32) if b == 0 else kh
            e_parts, denom_parts = [], []
            for g in range(GQA_GROUP):
                lanes = slice(g * BLK, (g + 1) * BLK)
                sh = jnp.where(from_prev, s[0:BLK, lanes], s[BLK:2 * BLK, lanes])
                sh = sh + bias_scr[table, :, lanes]
                sink = sink_scr[kh, :, lanes]
                m = jnp.maximum(jnp.max(sh, axis=0, keepdims=True), sink)
                e = jnp.exp2(sh - m)
                denom_parts.append(jnp.sum(e, axis=0, keepdims=True) + jnp.exp2(sink - m))
                e_parts.append(jnp.concatenate(
                    [jnp.where(from_prev, e, 0.0), jnp.where(from_prev, 0.0, e)],
                    axis=0).astype(BF16))
            e = jnp.concatenate(e_parts, axis=1)
            denom = jnp.concatenate(denom_parts, axis=1)
            o = jnp.dot(v_t[kh][:, band], e, preferred_element_type=F32)
            o = o * (1.0 / denom)
            out_t += [o[:, g * BLK:(g + 1) * BLK] for g in range(GQA_GROUP)]
        o_ref[rows, :] = jnp.concatenate(out_t, axis=0).T.astype(BF16)


def _attention(z, q_gain, k_gain, rel_bias, sinks, *, tq):
    s = z.shape[0]
    nblk = tq // BLK
    gw = GQA_GROUP * BLK
    q_col = POOL_W // ATTN_W
    kv_col = (POOL_W + ATTN_W) // KV_W
    bucket_t = _rel_bucket_band().T
    jm, ql = np.meshgrid(np.arange(BLK), np.arange(BLK), indexing="ij")
    bucket_fold = jnp.asarray(np.where(jm > ql, bucket_t[:BLK], bucket_t[BLK:]))
    return pl.pallas_call(
        functools.partial(_attn_kernel, nblk=nblk),
        grid=(s // tq,),
        in_specs=[
            pl.BlockSpec((tq, ATTN_W), lambda i: (i, q_col)),
            pl.BlockSpec((tq, KV_W), lambda i: (i, kv_col)),
            pl.BlockSpec((BLK, KV_W), lambda i: (jnp.maximum(i * nblk - 1, 0), kv_col)),
            pl.BlockSpec((HEAD_DIM, 1), lambda i: (0, 0)),
            pl.BlockSpec((HEAD_DIM, 1), lambda i: (0, 0)),
            pl.BlockSpec((BLK, BLK), lambda i: (0, 0)),
            pl.BlockSpec(memory_space=pltpu.SMEM),
            pl.BlockSpec(memory_space=pltpu.SMEM),
        ],
        out_specs=pl.BlockSpec((tq, ATTN_W), lambda i: (i, 0)),
        out_shape=jax.ShapeDtypeStruct((s, ATTN_W), BF16),
        scratch_shapes=[
            pltpu.VMEM((2 * N_KV_HEADS, BLK, gw), F32),
            pltpu.VMEM((N_KV_HEADS, 1, gw), F32),
        ],
        compiler_params=pltpu.CompilerParams(
            dimension_semantics=("arbitrary",), vmem_limit_bytes=VMEM_LIMIT),
        name="swa_attention",
    )(z, z, z, q_gain.reshape(HEAD_DIM, 1), k_gain.reshape(HEAD_DIM, 1), bucket_fold,
      rel_bias, sinks)


def _mix_kernel(u_ref, up_ref, ga_ref, gb_ref, attn_ref, x_ref, mod_ref, pmix_ref,
                pscale_ref, wpool_ref, wattn_ref, wo_ref, o_ref, ubuf, *, sub, tm):
    i = pl.program_id(0)
    d = o_ref.shape[1]
    halo = up_ref[...]
    ubuf[0:POOL_HALO, :] = jnp.where(i == 0, 0.0, halo)
    ubuf[POOL_HALO:, :] = u_ref[...]

    t1 = i * tm + lax.broadcasted_iota(jnp.int32, (tm, POOL_GROUP_W), 0) + 1
    mixed, merged_attn = [], []
    dc = d // len(POOL_WINDOWS)
    for gi, w in enumerate(POOL_WINDOWS):
        out_cols = slice(gi * dc, (gi + 1) * dc)
        y_attn = jnp.dot(attn_ref[...], wattn_ref[:, out_cols], preferred_element_type=F32)
        merged_attn.append(jax.nn.sigmoid(gb_ref[:, out_cols]) * y_attn)

        cols = slice(gi * POOL_GROUP_W, (gi + 1) * POOL_GROUP_W)
        cur = ubuf[POOL_HALO:POOL_HALO + tm, cols]
        win = cur
        for dlt in range(1, w):
            win = win + ubuf[POOL_HALO - dlt:POOL_HALO - dlt + tm, cols]
        cnt = jnp.minimum(t1, w).astype(F32)
        pooled = (win / cnt - cur).astype(BF16)
        mg = jnp.dot(pooled, pmix_ref[gi], preferred_element_type=F32)
        mixed.append((mg * pscale_ref[:, cols]).astype(BF16))
    mixed = jnp.concatenate(mixed, axis=1)

    y_pool = jnp.dot(mixed, wpool_ref[...], preferred_element_type=F32)
    merged = jax.nn.sigmoid(ga_ref[...]) * y_pool + jnp.concatenate(merged_attn, axis=1)
    y = jnp.dot(merged.astype(BF16), wo_ref[...], preferred_element_type=F32)
    gate = mod_ref[3 * sub + 2:3 * sub + 3, :]
    o_ref[...] = x_ref[...] + gate * y


def _mix(z, attn, x, mod, pool_mix, pool_scale, w_pool_up, w_attn_up, w_o, *, sub, tm):
    s, d = x.shape
    const = dict(pipeline_mode=pl.Buffered(1))
    ga_off = POOL_W + ATTN_W + KV_W
    return pl.pallas_call(
        functools.partial(_mix_kernel, sub=sub, tm=tm),
        grid=(s // tm,),
        in_specs=[
            pl.BlockSpec((tm, POOL_W), lambda i: (i, 0)),
            pl.BlockSpec((POOL_HALO, POOL_W),
                         lambda i: (jnp.maximum(i * (tm // POOL_HALO) - 1, 0), 0)),
            pl.BlockSpec((pl.Element(tm), pl.Element(d)), lambda i: (i * tm, ga_off)),
            pl.BlockSpec((pl.Element(tm), pl.Element(d)), lambda i: (i * tm, ga_off + d)),
            pl.BlockSpec((tm, ATTN_W), lambda i: (i, 0)),
            pl.BlockSpec((tm, d), lambda i: (i, 0)),
            pl.BlockSpec(mod.shape, lambda i: (0, 0), **const),
            pl.BlockSpec(pool_mix.shape, lambda i: (0, 0, 0), **const),
            pl.BlockSpec((1, POOL_W), lambda i: (0, 0), **const),
            pl.BlockSpec(w_pool_up.shape, lambda i: (0, 0), **const),
            pl.BlockSpec(w_attn_up.shape, lambda i: (0, 0), **const),
            pl.BlockSpec(w_o.shape, lambda i: (0, 0), **const),
        ],
        out_specs=pl.BlockSpec((tm, d), lambda i: (i, 0)),
        out_shape=jax.ShapeDtypeStruct((s, d), F32),
        scratch_shapes=[pltpu.VMEM((tm + POOL_HALO, POOL_W), F32)],
        compiler_params=pltpu.CompilerParams(
            dimension_semantics=("arbitrary",), vmem_limit_bytes=VMEM_LIMIT),
        name="pool_merge_out",
    )(z, z, z, z, attn, x, mod, pool_mix, pool_scale, w_pool_up, w_attn_up, w_o)


def kernel(x, c, w_ada, b_ada, g_ffn1, w_ffn1_gu, w_ffn1_down, g_mix, w_in, pool_mix,
           pool_scale, w_pool_up, q_gain, k_gain, sinks, rel_bias, w_attn_up, w_o,
           g_ffn2, w_ffn2_gu, w_ffn2_down):
    b, s, d = x.shape
    depth = w_ada.shape[0]
    assert b == 1, "adaLN matvec and row tiling assume a single sequence"
    xs = x[0]
    for l in range(depth):
        mod_1, sc_col = _ada(c.reshape(d, 1), w_ada, b_ada, layer=l, tn=1024, n_cols=3 * d)

        mod_1 = mod_1.reshape(3, d)
        h1 = _norm_rows(xs, mod_1, g_ffn1[l:l + 1], sub=0, tm=1024)
        a1, mod_rest, (down1, gu2, down2, w_in_b) = _ffn_up(
            h1, w_ffn1_gu, (w_ffn1_down, w_ffn2_gu, w_ffn2_down, w_in), sc_col, w_ada, b_ada,
            layer=l, sub=0, tm=1024, tf=512, row_parts=2, ada_from=3 * d)
        xs = _ffn_down(a1, down1, xs, mod_1, sub=0, tm=1024, tn=512)
        mod = jnp.concatenate([mod_1.reshape(1, 3 * d), mod_rest], axis=1).reshape(
            3 * N_SUB, d)

        z, (pmix_b, wpool_b, wattn_b, wo_b) = _inproj(
            xs, mod, g_mix[l:l + 1], w_in_b,
            (pool_mix.reshape(depth, POOL_W, POOL_GROUP_W), w_pool_up, w_attn_up, w_o),
            layer=l, sub=1, tm=1024, tn=1280, nchunk=4)
        attn = _attention(z, q_gain[l], k_gain[l], rel_bias, sinks[l], tq=512)
        xs = _mix(z, attn, xs, mod, pmix_b.reshape(pool_mix.shape[1:]), pool_scale[l:l + 1],
                  wpool_b, wattn_b, wo_b, sub=1, tm=256)

        xs = _ffn_pipe(xs, mod, g_ffn2[l:l + 1], gu2, down2, sub=2, tm=1024, tf=512,
                       nchunk=8, row_parts=2)
    return xs[None]
```
